```python
import math
import jax, jax.numpy as jnp
from jax import lax
import numpy as np

D_MODEL = 2048
BATCH = 4
SEQ = 2048
DEPTH = 1
DEC_BATCH = 128
DEC_SEQ = 4
PAST_LEN = 2048
PAGE_SIZE = 128

SB_HEADS = 8
SB_HEAD_DIM = 128
SB_WIDTH = SB_HEADS * SB_HEAD_DIM
SB_QBLOCK = 128
SB_SPAN_MIN = 4.0
SB_SPAN_MAX = 4096.0
GDN_HEADS = 8
GDN_KEY_DIM = 128
GDN_VAL_DIM = 128
GDN_QK_WIDTH = GDN_HEADS * GDN_KEY_DIM
GDN_V_WIDTH = GDN_HEADS * GDN_VAL_DIM
GDN_CONV = 4
GDN_CONV_CH = 2 * GDN_QK_WIDTH + GDN_V_WIDTH
GDN_CHUNK = 64
MIX_WIDTH = SB_WIDTH + GDN_V_WIDTH
N_MEM = 256
X_HEADS = 4
X_HEAD_DIM = 128
X_WIDTH = X_HEADS * X_HEAD_DIM
FFN_HIDDEN = -(-8 * D_MODEL // (3 * 256)) * 256
IN_SPLITS = (SB_WIDTH, SB_WIDTH, SB_WIDTH, GDN_QK_WIDTH, GDN_QK_WIDTH, GDN_V_WIDTH, GDN_HEADS, GDN_HEADS, GDN_V_WIDTH)
IN_WIDTH = 3 * SB_WIDTH + 2 * GDN_QK_WIDTH + 2 * GDN_V_WIDTH + 2 * GDN_HEADS
EPS = 1e-6

kernel_name = 'hybrid_stickbreak_gdeltanet_decode_step'


def split_cols(t, sizes):
    idx, acc = [], 0
    for s in sizes[:-1]:
        acc += s
        idx.append(acc)
    return jnp.split(t, idx, axis=-1)


def rmsnorm(x, g):
    xf = x.astype(jnp.float32)
    y = xf * lax.rsqrt(jnp.mean(xf * xf, axis=-1, keepdims=True) + EPS)
    return (y * g.astype(jnp.float32)).astype(x.dtype)


def l2norm(x):
    xf = x.astype(jnp.float32)
    return (xf * lax.rsqrt(jnp.sum(xf * xf, axis=-1, keepdims=True) + EPS)).astype(x.dtype)


def stick_breaking_attention(q, k, v, q_pos, k_pos, bias):
    b, sq, h, d = q.shape
    blk = math.gcd(sq, SB_QBLOCK)
    nb = sq // blk
    scale = d ** -0.5
    qb = q.reshape(b, nb, blk, h, d).transpose(1, 0, 2, 3, 4)
    pb = q_pos.reshape(nb, blk)
    bh = bias.astype(jnp.float32)[None, :, None, None]

    def block(args):
        qi, pi = args
        z = jnp.einsum('bqhd,bkhd->bhqk', qi, k, preferred_element_type=jnp.float32) * scale + bh
        causal = (k_pos[None, :] < pi[:, None])[None, None]
        log_stay = jnp.where(causal, jax.nn.log_sigmoid(-z), 0.0)
        survive = lax.cumsum(log_stay, axis=3, reverse=True) - log_stay
        w = jnp.exp(jnp.where(causal, jax.nn.log_sigmoid(z) + survive, -jnp.inf))
        return jnp.einsum('bhqk,bkhd->bqhd', w.astype(v.dtype), v)

    o = lax.map(block, (qb, pb))
    return o.transpose(1, 0, 2, 3, 4).reshape(b, sq, h, d)


def gated_delta_rule(q, k, v, g, beta, s0):
    b, s, h, dk = q.shape
    dv = v.shape[-1]
    c = math.gcd(s, GDN_CHUNK)
    n = s // c
    f32 = jnp.float32

    def to_chunks(t):
        t = t.astype(f32).reshape(b, n, c, h, *t.shape[3:])
        return jnp.moveaxis(t, 3, 2).swapaxes(0, 1)

    qc, kc, vc, gc, bc = (to_chunks(t) for t in (q, k, v, g, beta))
    cum = jnp.cumsum(gc, axis=-1)
    ii = jnp.arange(c)
    incl = ii[:, None] >= ii[None, :]
    strict = ii[:, None] > ii[None, :]
    decay = jnp.exp(jnp.where(incl, cum[..., :, None] - cum[..., None, :], -jnp.inf))
    kk = jnp.einsum('nbhid,nbhjd->nbhij', kc, kc)
    lower = jnp.where(strict, bc[..., :, None] * kk * decay, 0.0)
    rhs = jnp.concatenate([vc * bc[..., None], kc * (bc * jnp.exp(cum))[..., None]], axis=-1)
    sol = lax.linalg.triangular_solve(lower + jnp.eye(c, dtype=f32), rhs, left_side=True, lower=True, unit_diagonal=True)
    u, w = sol[..., :dv], sol[..., dv:]
    qk = jnp.einsum('nbhid,nbhjd->nbhij', qc, kc) * decay
    q_dec = qc * jnp.exp(cum)[..., None]
    k_end = kc * jnp.exp(cum[..., -1:] - cum)[..., None]
    g_end = jnp.exp(cum[..., -1])[..., None, None]

    def step(state, inp):
        u_i, w_i, qk_i, q_i, k_i, g_i = inp
        v_new = u_i - jnp.einsum('bhcd,bhde->bhce', w_i, state)
        o_i = jnp.einsum('bhcd,bhde->bhce', q_i, state) + jnp.einsum('bhij,bhje->bhie', qk_i, v_new)
        state = g_i * state + jnp.einsum('bhcd,bhce->bhde', k_i, v_new)
        return state, o_i

    s_fin, o = lax.scan(step, s0.astype(f32), (u, w, qk, q_dec, k_end, g_end))
    o = jnp.moveaxis(o.swapaxes(0, 1), 2, 3).reshape(b, s, h, dv)
    return o, s_fin


def memory_kv(mem, p):
    b, m, _ = mem.shape
    hm = rmsnorm(mem, p['norm_mem_g'])
    mk = rmsnorm((hm @ p['w_mk']).reshape(b, m, X_HEADS, X_HEAD_DIM), p['x_k_norm_g'])
    mv = (hm @ p['w_mv']).reshape(b, m, X_HEADS, X_HEAD_DIM)
    return mk, mv


def decoder_layer(x, mem_k, mem_v, sb_k_past, sb_v_past, gdn_s0, conv0, pos0, p):
    b, s, _ = x.shape
    h = rmsnorm(x, p['norm_mix_g'])
    sq, sk, sv, gq, gk, gv, ga, gb, gz = split_cols(h @ p['w_in'], IN_SPLITS)
    q = rmsnorm(sq.reshape(b, s, SB_HEADS, SB_HEAD_DIM), p['sb_q_norm_g'])
    k_new = rmsnorm(sk.reshape(b, s, SB_HEADS, SB_HEAD_DIM), p['sb_k_norm_g'])
    v_new = sv.reshape(b, s, SB_HEADS, SB_HEAD_DIM)
    k_all = jnp.concatenate([sb_k_past.astype(k_new.dtype), k_new], axis=1)
    v_all = jnp.concatenate([sb_v_past.astype(v_new.dtype), v_new], axis=1)
    q_pos = pos0 + jnp.arange(s, dtype=jnp.int32)
    k_pos = jnp.arange(k_all.shape[1], dtype=jnp.int32)
    o_sb = stick_breaking_attention(q, k_all, v_all, q_pos, k_pos, p['sb_logit_bias']).reshape(b, s, SB_WIDTH)
    raw = jnp.concatenate([gq, gk, gv], axis=-1)
    padded = jnp.concatenate([conv0.astype(raw.dtype), raw], axis=1)
    conv = sum(padded[:, j:j + s] * p['gdn_conv_w'][j] for j in range(GDN_CONV))
    conv_new = padded[:, s:]
    cq, ck, cv = split_cols(jax.nn.silu(conv), (GDN_QK_WIDTH, GDN_QK_WIDTH, GDN_V_WIDTH))
    gdn_q = l2norm(cq.reshape(b, s, GDN_HEADS, GDN_KEY_DIM)) * (GDN_KEY_DIM ** -0.5)
    gdn_k = l2norm(ck.reshape(b, s, GDN_HEADS, GDN_KEY_DIM))
    gdn_v = cv.reshape(b, s, GDN_HEADS, GDN_VAL_DIM)
    beta = jax.nn.sigmoid(gb.astype(jnp.float32))
    log_decay = -jnp.exp(p['gdn_a_log'].astype(jnp.float32)) * jax.nn.softplus(ga.astype(jnp.float32) + p['gdn_dt_bias'].astype(jnp.float32))
    o_gdn, gdn_s = gated_delta_rule(gdn_q, gdn_k, gdn_v, log_decay, beta, gdn_s0)
    o_gdn = rmsnorm(o_gdn.astype(x.dtype), p['gdn_out_norm_g']) * jax.nn.silu(gz.reshape(b, s, GDN_HEADS, GDN_VAL_DIM))
    x = x + jnp.concatenate([o_sb, o_gdn.reshape(b, s, GDN_V_WIDTH)], axis=-1) @ p['w_out']
    hx = rmsnorm(x, p['norm_x_g'])
    xq = rmsnorm((hx @ p['w_xq']).reshape(b, s, X_HEADS, X_HEAD_DIM), p['x_q_norm_g'])
    logits = jnp.einsum('bshd,bmhd->bhsm', xq, mem_k, preferred_element_type=jnp.float32) * (X_HEAD_DIM ** -0.5)
    attn = jax.nn.softmax(logits, axis=-1).astype(mem_v.dtype)
    xo = jnp.einsum('bhsm,bmhd->bshd', attn, mem_v).reshape(b, s, X_WIDTH)
    x = x + xo @ p['w_xo']
    hf = rmsnorm(x, p['norm_ffn_g'])
    gate, up = jnp.split(hf @ p['w_gate_up'], 2, axis=-1)
    x = x + (jax.nn.silu(gate) * up) @ p['w_down']
    return x, k_new, v_new, gdn_s.astype(gdn_s0.dtype), conv_new


def setup_inputs(seed: int = 0) -> dict:
    key = jax.random.key(seed)
    k = jax.random.split(key, 32)
    f32 = jnp.float32

    def normal(i, shape, scale):
        return jax.random.normal(k[i], shape, f32) * scale

    def gain(i, n):
        return 1.0 + 0.02 * jax.random.normal(k[i], (DEPTH, n), f32)

    n_pages = PAST_LEN // PAGE_SIZE
    n_used = DEC_BATCH * n_pages
    n_phys = n_used + max(1, n_used // 4)
    page_table = jax.random.permutation(k[5], n_phys)[:n_used].reshape(DEC_BATCH, n_pages).astype(jnp.int32)
    dt = jnp.exp(jax.random.uniform(k[14], (DEPTH, GDN_HEADS), f32, math.log(1e-3), math.log(1e-1)))
    dt_bias = dt + jnp.log(-jnp.expm1(-dt))
    a_log = jnp.log(jax.random.uniform(k[15], (DEPTH, GDN_HEADS), f32, 1.0, 16.0))
    span = jnp.exp(jnp.linspace(math.log(SB_SPAN_MIN), math.log(SB_SPAN_MAX), SB_HEADS, dtype=f32))
    sb_bias = -jnp.log(span)[None, :] + 0.05 * jax.random.normal(k[30], (DEPTH, SB_HEADS), f32)
    return {
        'x_prompt': normal(0, (BATCH, SEQ, D_MODEL), 1.0),
        'x_sample': normal(1, (DEC_BATCH, DEC_SEQ, D_MODEL), 1.0),
        'mem_prompt': normal(2, (BATCH, N_MEM, D_MODEL), 1.0),
        'cache_sb_k': normal(3, (DEPTH, n_phys, PAGE_SIZE, SB_HEADS, SB_HEAD_DIM), 1.0),
        'cache_sb_v': normal(4, (DEPTH, n_phys, PAGE_SIZE, SB_HEADS, SB_HEAD_DIM), 1.0),
        'page_table': page_table,
        'state_gdn': normal(6, (DEPTH, DEC_BATCH, GDN_HEADS, GDN_KEY_DIM, GDN_VAL_DIM), 0.1),
        'state_gdn_conv': normal(7, (DEPTH, DEC_BATCH, GDN_CONV - 1, GDN_CONV_CH), 1.0),
        'cache_mem_k': normal(8, (DEPTH, DEC_BATCH, N_MEM, X_HEADS, X_HEAD_DIM), 1.0),
        'cache_mem_v': normal(9, (DEPTH, DEC_BATCH, N_MEM, X_HEADS, X_HEAD_DIM), 1.0),
        'norm_mix_g': gain(10, D_MODEL),
        'w_in': normal(11, (DEPTH, D_MODEL, IN_WIDTH), D_MODEL ** -0.5),
        'sb_q_norm_g': gain(12, SB_HEAD_DIM),
        'sb_k_norm_g': gain(13, SB_HEAD_DIM),
        'sb_logit_bias': sb_bias,
        'gdn_conv_w': normal(16, (DEPTH, GDN_CONV, GDN_CONV_CH), GDN_CONV ** -0.5),
        'gdn_a_log': a_log,
        'gdn_dt_bias': dt_bias,
        'gdn_out_norm_g': gain(17, GDN_VAL_DIM),
        'w_out': normal(18, (DEPTH, MIX_WIDTH, D_MODEL), MIX_WIDTH ** -0.5),
        'norm_x_g': gain(19, D_MODEL),
        'norm_mem_g': gain(20, D_MODEL),
        'w_xq': normal(21, (DEPTH, D_MODEL, X_WIDTH), D_MODEL ** -0.5),
        'w_mk': normal(22, (DEPTH, D_MODEL, X_WIDTH), D_MODEL ** -0.5),
        'w_mv': normal(23, (DEPTH, D_MODEL, X_WIDTH), D_MODEL ** -0.5),
        'x_q_norm_g': gain(24, X_HEAD_DIM),
        'x_k_norm_g': gain(25, X_HEAD_DIM),
        'w_xo': normal(26, (DEPTH, X_WIDTH, D_MODEL), X_WIDTH ** -0.5),
        'norm_ffn_g': gain(27, D_MODEL),
        'w_gate_up': normal(28, (DEPTH, D_MODEL, 2 * FFN_HIDDEN), D_MODEL ** -0.5),
        'w_down': normal(29, (DEPTH, FFN_HIDDEN, D_MODEL), FFN_HIDDEN ** -0.5),
    }


def reference(x_prompt, x_sample, mem_prompt, cache_sb_k, cache_sb_v, page_table,
              state_gdn, state_gdn_conv, cache_mem_k, cache_mem_v,
              norm_mix_g, w_in, sb_q_norm_g, sb_k_norm_g, sb_logit_bias, gdn_conv_w, gdn_a_log,
              gdn_dt_bias, gdn_out_norm_g, w_out, norm_x_g, norm_mem_g, w_xq, w_mk, w_mv, x_q_norm_g,
              x_k_norm_g, w_xo, norm_ffn_g, w_gate_up, w_down):
    bp = x_prompt.shape[0]
    bs = x_sample.shape[0]
    past_len = page_table.shape[1] * PAGE_SIZE
    dt = x_prompt.dtype
    yp, ys = x_prompt, x_sample
    sbk_p, sbv_p, sbk_s, sbv_s = [], [], [], []
    gs_p, gc_p, gs_s, gc_s, mk_p, mv_p = [], [], [], [], [], []
    for l in range(DEPTH):
        p = {
            'norm_mix_g': norm_mix_g[l], 'w_in': w_in[l], 'sb_q_norm_g': sb_q_norm_g[l],
            'sb_k_norm_g': sb_k_norm_g[l], 'sb_logit_bias': sb_logit_bias[l], 'gdn_conv_w': gdn_conv_w[l],
            'gdn_a_log': gdn_a_log[l], 'gdn_dt_bias': gdn_dt_bias[l], 'gdn_out_norm_g': gdn_out_norm_g[l],
            'w_out': w_out[l], 'norm_x_g': norm_x_g[l], 'norm_mem_g': norm_mem_g[l], 'w_xq': w_xq[l],
            'w_mk': w_mk[l], 'w_mv': w_mv[l], 'x_q_norm_g': x_q_norm_g[l], 'x_k_norm_g': x_k_norm_g[l],
            'w_xo': w_xo[l], 'norm_ffn_g': norm_ffn_g[l], 'w_gate_up': w_gate_up[l], 'w_down': w_down[l],
        }
        mem_k, mem_v = memory_kv(mem_prompt, p)
        empty = jnp.zeros((bp, 0, SB_HEADS, SB_HEAD_DIM), dt)
        s0 = jnp.zeros((bp, GDN_HEADS, GDN_KEY_DIM, GDN_VAL_DIM), state_gdn.dtype)
        c0 = jnp.zeros((bp, GDN_CONV - 1, GDN_CONV_CH), state_gdn_conv.dtype)
        yp, kp, vp, sp_, cp = decoder_layer(yp, mem_k, mem_v, empty, empty, s0, c0, 0, p)
        sbk_p.append(kp); sbv_p.append(vp); gs_p.append(sp_); gc_p.append(cp); mk_p.append(mem_k); mv_p.append(mem_v)
        past_k = cache_sb_k[l][page_table].reshape(bs, past_len, SB_HEADS, SB_HEAD_DIM)
        past_v = cache_sb_v[l][page_table].reshape(bs, past_len, SB_HEADS, SB_HEAD_DIM)
        ys, ks_, vs_, ss_, cs_ = decoder_layer(ys, cache_mem_k[l], cache_mem_v[l], past_k, past_v,
                                               state_gdn[l], state_gdn_conv[l], past_len, p)
        sbk_s.append(ks_); sbv_s.append(vs_); gs_s.append(ss_); gc_s.append(cs_)
    return (yp, ys, jnp.stack(sbk_p), jnp.stack(sbv_p), jnp.stack(sbk_s), jnp.stack(sbv_s),
            jnp.stack(gs_p), jnp.stack(gc_p), jnp.stack(gs_s), jnp.stack(gc_s),
            jnp.stack(mk_p), jnp.stack(mv_p))
```

```python
import functools
import math

import jax
import jax.numpy as jnp
from jax import lax
from jax.experimental import pallas as pl
from jax.experimental.pallas import tpu as pltpu

F32 = jnp.float32
BF16 = jnp.bfloat16
EPS = 1e-6
HEAD = 128
SB_HEADS = 8
GDN_HEADS = 8
X_HEADS = 4
PAGE = 128
GDN_CONV = 4
GDN_CHUNK = 64
MIB = 1 << 20


def _cparams(sem, vmem_mib):
    return pltpu.CompilerParams(dimension_semantics=sem, vmem_limit_bytes=vmem_mib * MIB)


def _dot(a, b):
    return jnp.dot(a, b, preferred_element_type=F32)


def _dot_nt(a, b):
    return lax.dot_general(a, b, (((1,), (1,)), ((), ())), preferred_element_type=F32)


def _dot_tn(a, b):
    return lax.dot_general(a, b, (((0,), (0,)), ((), ())), preferred_element_type=F32)


def _split2(a):
    hi = a.astype(BF16)
    lo = (a - hi.astype(F32)).astype(BF16)
    return hi, lo


def _dot_hp(a, b, dot=_dot):
    ah, al = _split2(a)
    bh, bl = _split2(b)
    return dot(ah, bh) + (dot(ah, bl) + dot(al, bh))


def _dot_exact_lhs(m01, b):
    b0 = b.astype(BF16)
    r1 = b - b0.astype(F32)
    b1 = r1.astype(BF16)
    b2 = (r1 - b1.astype(F32)).astype(BF16)
    return _dot(m01, b0) + (_dot(m01, b1) + _dot(m01, b2))


def _dot_exact_rhs(a, m01):
    hi, lo = _split2(a)
    return _dot(hi, m01) + _dot(lo, m01)


def _softplus(z):
    return jnp.maximum(z, 0.0) + jnp.log1p(jnp.exp(-jnp.abs(z)))


def _silu(x):
    return x * jax.nn.sigmoid(x)


def _rms(y, g):
    ms = jnp.mean(y * y, axis=-1, keepdims=True)
    return y * lax.rsqrt(ms + EPS) * g


def _rmsnorm_kernel(x_ref, g_ref, o_ref):
    o_ref[...] = _rms(x_ref[...], g_ref[...]).astype(o_ref.dtype)


def _rmsnorm_bf16(x, g):
    t, d = x.shape
    tm = min(t, 512)
    return pl.pallas_call(
        _rmsnorm_kernel,
        grid=(t // tm,),
        in_specs=[pl.BlockSpec((tm, d), lambda i: (i, 0)), pl.BlockSpec((1, d), lambda i: (0, 0))],
        out_specs=pl.BlockSpec((tm, d), lambda i: (i, 0)),
        out_shape=jax.ShapeDtypeStruct((t, d), BF16),
        compiler_params=_cparams(("parallel",), 32),
        name="rmsnorm",
    )(x, g.reshape(1, d))


def _proj_kernel(x_ref, w_ref, g_ref, o_ref, *, head_norm):
    acc = _dot(x_ref[...], w_ref[...].astype(BF16))
    if head_norm:
        g = g_ref[...]
        for s in range(acc.shape[1] // HEAD):
            sl = slice(s * HEAD, (s + 1) * HEAD)
            o_ref[:, sl] = _rms(acc[:, sl], g).astype(o_ref.dtype)
    else:
        o_ref[...] = acc.astype(o_ref.dtype)


def _proj(h, w, col0, n_cols, out_dtype, head_gain=None, tn=512):
    t, k = h.shape
    tn = min(tn, n_cols)
    tm = min(t, 1024)
    assert col0 % tn == 0 and n_cols % tn == 0 and t % tm == 0
    cb = col0 // tn
    g = jnp.ones((1, HEAD), F32) if head_gain is None else head_gain.reshape(1, HEAD).astype(F32)
    return pl.pallas_call(
        functools.partial(_proj_kernel, head_norm=head_gain is not None),
        grid=(t // tm, n_cols // tn),
        in_specs=[pl.BlockSpec((tm, k), lambda i, j: (i, 0)),
                  pl.BlockSpec((k, tn), lambda i, j: (0, cb + j)),
                  pl.BlockSpec((1, HEAD), lambda i, j: (0, 0))],
        out_specs=pl.BlockSpec((tm, tn), lambda i, j: (i, j)),
        out_shape=jax.ShapeDtypeStruct((t, n_cols), out_dtype),
        compiler_params=_cparams(("parallel", "arbitrary"), 48),
        name="proj",
    )(h, w, g)


def _sb_prompt_kernel(bias_ref, q_ref, k_ref, v_ref, o_ref, *, tq, scale):
    h = pl.program_id(1)
    i = pl.program_id(2)
    q = q_ref[0]
    bias = bias_ref[h]
    row = lax.broadcasted_iota(jnp.int32, (tq, tq), 0)
    col = lax.broadcasted_iota(jnp.int32, (tq, tq), 1)
    later = (row > col).astype(BF16)

    def body(jj, carry):
        csum, acc = carry
        j = i - jj
        start = pl.multiple_of(j * tq, tq)
        kb = k_ref[0, pl.ds(start, tq), :].astype(BF16)
        vb = v_ref[0, pl.ds(start, tq), :].astype(BF16)
        z = _dot_nt(q, kb) * scale + bias
        sp = _softplus(z)
        causal = (col + j * tq) < (row + i * tq)
        log_stay = jnp.where(causal, -sp, 0.0)
        survive = _dot_exact_rhs(log_stay, later) + csum
        w = jnp.where(causal, jnp.exp(z - sp + survive), 0.0)
        acc = acc + _dot(w.astype(BF16), vb)
        csum = csum + jnp.sum(log_stay, axis=-1, keepdims=True)
        return csum, acc

    init = (jnp.zeros((tq, 1), F32), jnp.zeros((tq, HEAD), F32))
    _, acc = lax.fori_loop(0, i + 1, body, init)
    o_ref[0] = acc.astype(o_ref.dtype)


def _sb_prompt(q, k, v, bias, b, s):
    width = q.shape[1]
    nh = width // HEAD
    tq = min(s, 256)
    assert s % tq == 0
    q3, k3, v3 = (a.reshape(b, s, width) for a in (q, k, v))
    out = pl.pallas_call(
        functools.partial(_sb_prompt_kernel, tq=tq, scale=HEAD ** -0.5),
        grid=(b, nh, s // tq),
        in_specs=[pl.BlockSpec(memory_space=pltpu.SMEM),
                  pl.BlockSpec((1, tq, HEAD), lambda bi, h, i: (bi, i, h)),
                  pl.BlockSpec((1, s, HEAD), lambda bi, h, i: (bi, 0, h)),
                  pl.BlockSpec((1, s, HEAD), lambda bi, h, i: (bi, 0, h))],
        out_specs=pl.BlockSpec((1, tq, HEAD), lambda bi, h, i: (bi, i, h)),
        out_shape=jax.ShapeDtypeStruct((b, s, width), BF16),
        compiler_params=_cparams(("parallel", "parallel", "arbitrary"), 32),
        name="sb_prompt",
    )(bias.astype(F32), q3, k3, v3)
    return out.reshape(b * s, width)


def _sb_sample_kernel(pt_ref, q_ref, bias_ref, knew_ref, vnew_ref, *refs, n_pages, n_new, scale):
    del pt_ref
    k_pages = refs[:n_pages]
    v_pages = refs[n_pages:2 * n_pages]
    o_ref, kbf, vbf = refs[2 * n_pages:]
    past = n_pages * PAGE
    width = kbf.shape[1]
    rows = q_ref.shape[1]
    new_rows = knew_ref.shape[1]
    for p in range(n_pages):
        kbf[p * PAGE:(p + 1) * PAGE, :] = k_pages[p][0].astype(BF16)
        vbf[p * PAGE:(p + 1) * PAGE, :] = v_pages[p][0].astype(BF16)
    kbf[past:past + PAGE, :] = jnp.zeros((PAGE, width), BF16)
    vbf[past:past + PAGE, :] = jnp.zeros((PAGE, width), BF16)
    kbf[past:past + new_rows, :] = knew_ref[0].astype(BF16)
    vbf[past:past + new_rows, :] = vnew_ref[0].astype(BF16)

    q = q_ref[0]
    r_w = lax.broadcasted_iota(jnp.int32, (rows, width), 0)
    c_w = lax.broadcasted_iota(jnp.int32, (rows, width), 1)
    own_head = (r_w & (SB_HEADS - 1)) == (c_w >> 7)
    qbd = jnp.where(own_head, jnp.concatenate([q] * (width // HEAD), axis=1), 0.0).astype(BF16)

    total = past + PAGE
    z = _dot_nt(qbd, kbf[...]) * scale + bias_ref[...]
    sp = _softplus(z)
    r_k = lax.broadcasted_iota(jnp.int32, (rows, total), 0)
    c_k = lax.broadcasted_iota(jnp.int32, (rows, total), 1)
    t_of_row = r_k >> 3
    valid = (c_k < past) | (((c_k - past) < t_of_row) & ((c_k - past) < n_new))
    log_stay = jnp.where(valid, -sp, 0.0)

    rr = lax.broadcasted_iota(jnp.int32, (PAGE, PAGE), 0)
    cc = lax.broadcasted_iota(jnp.int32, (PAGE, PAGE), 1)
    later = (rr > cc).astype(BF16)
    csum = jnp.zeros((rows, 1), F32)
    pieces = []
    for blk in reversed(range(total // PAGE)):
        lsb = log_stay[:, blk * PAGE:(blk + 1) * PAGE]
        pieces.append(_dot_exact_rhs(lsb, later) + csum)
        csum = csum + jnp.sum(lsb, axis=-1, keepdims=True)
    survive = jnp.concatenate(pieces[::-1], axis=1)
    w = jnp.where(valid, jnp.exp(z - sp + survive), 0.0).astype(BF16)
    o_full = _dot(w, vbf[...])
    o_sel = jnp.where(own_head, o_full, 0.0)
    o_ref[0] = jnp.sum(o_sel.reshape(rows // SB_HEADS, SB_HEADS, width), axis=1)


def _sb_sample(q, k_new, v_new, bias, cache_k, cache_v, page_table, b, s):
    width = q.shape[1]
    nh = width // HEAD
    assert nh == SB_HEADS
    n_pages = page_table.shape[1]
    n_phys = cache_k.shape[0]
    ck = cache_k.reshape(n_phys, PAGE, width)
    cv = cache_v.reshape(n_phys, PAGE, width)
    new_rows = 16
    q3 = q.reshape(b, s * nh, HEAD)
    kn = jnp.pad(k_new.reshape(b, s, width), ((0, 0), (0, new_rows - s), (0, 0)))
    vn = jnp.pad(v_new.reshape(b, s, width), ((0, 0), (0, new_rows - s), (0, 0)))
    bias_rows = jnp.tile(bias.astype(F32), s).reshape(s * nh, 1)

    def page_spec(p):
        return pl.BlockSpec((1, PAGE, width), lambda bi, pt: (pt[bi, p], 0, 0))

    grid_spec = pltpu.PrefetchScalarGridSpec(
        num_scalar_prefetch=1,
        grid=(b,),
        in_specs=[pl.BlockSpec((1, s * nh, HEAD), lambda bi, pt: (bi, 0, 0)),
                  pl.BlockSpec((s * nh, 1), lambda bi, pt: (0, 0)),
                  pl.BlockSpec((1, new_rows, width), lambda bi, pt: (bi, 0, 0)),
                  pl.BlockSpec((1, new_rows, width), lambda bi, pt: (bi, 0, 0))]
        + [page_spec(p) for p in range(n_pages)] * 2,
        out_specs=pl.BlockSpec((1, s, width), lambda bi, pt: (bi, 0, 0)),
        scratch_shapes=[pltpu.VMEM((n_pages * PAGE + PAGE, width), BF16),
                        pltpu.VMEM((n_pages * PAGE + PAGE, width), BF16)],
    )
    out = pl.pallas_call(
        functools.partial(_sb_sample_kernel, n_pages=n_pages, n_new=s, scale=HEAD ** -0.5),
        grid_spec=grid_spec,
        out_shape=jax.ShapeDtypeStruct((b, s, width), F32),
        compiler_params=_cparams(("arbitrary",), 56),
        name="sb_sample",
    )(page_table, q3, bias_rows, kn, vn, *([ck] * n_pages), *([cv] * n_pages))
    return out.reshape(b * s, width)


def _gdn_kernel(raw_ref, ab_ref, gz_ref, convw_ref, conv0_ref, s0_ref, alog_ref, dtb_ref, gn_ref,
                o_ref, s_ref, pad_ref, *, c, n_valid):
    n = pl.program_id(1)
    nh = s_ref.shape[1]
    qkw = nh * HEAD

    @pl.when(n == 0)
    def _():
        pad_ref[0:8, :] = jnp.zeros((8, pad_ref.shape[1]), F32)
        pad_ref[5:8, :] = conv0_ref[0]
        s_ref[...] = s0_ref[...]

    pad_ref[8:8 + c, :] = raw_ref[0]
    cw = convw_ref[...]
    conv = pad_ref[5:5 + c, :] * cw[0:1, :]
    for j in range(1, GDN_CONV):
        conv = conv + pad_ref[5 + j:5 + j + c, :] * cw[j:j + 1, :]
    pad_ref[5:8, :] = pad_ref[5 + c:8 + c, :]
    xs = _silu(conv)

    ab = ab_ref[0]
    live = lax.broadcasted_iota(jnp.int32, ab.shape, 0) < n_valid
    g_all = jnp.where(live, -jnp.exp(alog_ref[...]) * _softplus(ab + dtb_ref[...]), 0.0)
    beta_all = jnp.where(live, jax.nn.sigmoid(ab), 0.0)
    ri = lax.broadcasted_iota(jnp.int32, (c, c), 0)
    ci = lax.broadcasted_iota(jnp.int32, (c, c), 1)
    upto = (ci <= ri).astype(BF16)
    cum_all = _dot_exact_lhs(upto, g_all)
    cum_last = cum_all[c - 1:c, :]
    ecum_all = jnp.exp(cum_all)
    eend_all = jnp.exp(cum_last - cum_all)
    glast_all = jnp.exp(cum_last)
    eye = (ri == ci).astype(F32)
    gn = gn_ref[...]

    for h in range(nh):
        sl = slice(h * HEAD, (h + 1) * HEAD)
        qh = xs[:, sl]
        kh = xs[:, qkw + h * HEAD:qkw + (h + 1) * HEAD]
        vh = xs[:, 2 * qkw + h * HEAD:2 * qkw + (h + 1) * HEAD]
        qh = qh * lax.rsqrt(jnp.sum(qh * qh, axis=-1, keepdims=True) + EPS) * (HEAD ** -0.5)
        kh = kh * lax.rsqrt(jnp.sum(kh * kh, axis=-1, keepdims=True) + EPS)
        g = g_all[:, h:h + 1]
        beta = beta_all[:, nh + h:nh + h + 1]
        ecum = ecum_all[:, h:h + 1]
        eend = eend_all[:, h:h + 1]
        glast = glast_all[:, h:h + 1]
        diff = _dot_exact_lhs(upto, jnp.where(ri > ci, jnp.broadcast_to(g, (c, c)), 0.0))
        decay = jnp.where(ri >= ci, jnp.exp(diff), 0.0)
        kk = _dot_hp(kh, kh, _dot_nt)
        lower = jnp.where(ri > ci, beta * kk * decay, 0.0)
        inv = eye - lower
        npow = lower
        for _ in range(int(math.log2(c)) - 1):
            npow = _dot_hp(npow, npow)
            inv = inv + _dot_hp(inv, npow)
        rhs = jnp.concatenate([vh * beta, kh * (beta * ecum)], axis=-1)
        sol = _dot_hp(inv, rhs)
        u = sol[:, :HEAD]
        wmat = sol[:, HEAD:]
        qk = _dot_hp(qh, kh, _dot_nt) * decay
        state = s_ref[0, h]
        v_new = u - _dot_hp(wmat, state)
        o = _dot_hp(qh * ecum, state) + _dot_hp(qk, v_new)
        s_ref[0, h] = glast * state + _dot_hp(kh * eend, v_new, _dot_tn)
        gate = _silu(gz_ref[0, :, sl])
        o_ref[0, :, sl] = (_rms(o, gn) * gate).astype(o_ref.dtype)


def _gdn(raw, ab, gz, conv_w, conv0, s0, a_log, dt_bias, out_gain, b, s, c, n_valid):
    nh = s0.shape[1]
    ch = raw.shape[2]
    pad_lane = lambda a: jnp.pad(a.astype(F32), (0, HEAD - a.shape[0])).reshape(1, HEAD)
    out, state = pl.pallas_call(
        functools.partial(_gdn_kernel, c=c, n_valid=n_valid),
        grid=(b, s // c),
        in_specs=[pl.BlockSpec((1, c, ch), lambda bi, n: (bi, n, 0)),
                  pl.BlockSpec((1, c, HEAD), lambda bi, n: (bi, n, 0)),
                  pl.BlockSpec((1, c, nh * HEAD), lambda bi, n: (bi, n, 0)),
                  pl.BlockSpec((GDN_CONV, ch), lambda bi, n: (0, 0)),
                  pl.BlockSpec((1, GDN_CONV - 1, ch), lambda bi, n: (bi, 0, 0)),
                  pl.BlockSpec((1, nh, HEAD, HEAD), lambda bi, n: (bi, 0, 0, 0)),
                  pl.BlockSpec((1, HEAD), lambda bi, n: (0, 0)),
                  pl.BlockSpec((1, HEAD), lambda bi, n: (0, 0)),
                  pl.BlockSpec((1, HEAD), lambda bi, n: (0, 0))],
        out_specs=[pl.BlockSpec((1, c, nh * HEAD), lambda bi, n: (bi, n, 0)),
                   pl.BlockSpec((1, nh, HEAD, HEAD), lambda bi, n: (bi, 0, 0, 0))],
        out_shape=[jax.ShapeDtypeStruct((b, s, nh * HEAD), BF16),
                   jax.ShapeDtypeStruct(s0.shape, F32)],
        scratch_shapes=[pltpu.VMEM((8 + c, ch), F32)],
        compiler_params=_cparams(("parallel", "arbitrary"), 32),
        name="gdn",
    )(raw, ab, gz, conv_w, conv0, s0, pad_lane(a_log), pad_lane(dt_bias), out_gain.reshape(1, HEAD))
    return out, state


def _out_kernel(*refs, n_a, with_norm):
    a_refs = refs[:n_a]
    w_ref, x_ref, g_ref = refs[n_a:n_a + 3]
    outs = refs[n_a + 3:]
    acc = x_ref[...]
    off = 0
    for a in a_refs:
        ka = a.shape[1]
        acc = acc + _dot(a[...].astype(BF16), w_ref[off:off + ka, :])
        off += ka
    outs[0][...] = acc
    if with_norm:
        outs[1][...] = _rms(acc, g_ref[...]).astype(outs[1].dtype)


def _proj_residual(acts, w_bf16, x, next_gain=None):
    t, d = x.shape
    tm = min(t, 256)
    with_norm = next_gain is not None
    g = (next_gain if with_norm else jnp.ones((d,), F32)).reshape(1, d)
    out_shape = [jax.ShapeDtypeStruct((t, d), F32)]
    out_specs = [pl.BlockSpec((tm, d), lambda i: (i, 0))]
    if with_norm:
        out_shape.append(jax.ShapeDtypeStruct((t, d), BF16))
        out_specs.append(pl.BlockSpec((tm, d), lambda i: (i, 0)))
    res = pl.pallas_call(
        functools.partial(_out_kernel, n_a=len(acts), with_norm=with_norm),
        grid=(t // tm,),
        in_specs=[pl.BlockSpec((tm, a.shape[1]), lambda i: (i, 0)) for a in acts]
        + [pl.BlockSpec(w_bf16.shape, lambda i: (0, 0)),
           pl.BlockSpec((tm, d), lambda i: (i, 0)),
           pl.BlockSpec((1, d), lambda i: (0, 0))],
        out_specs=out_specs,
        out_shape=out_shape,
        compiler_params=_cparams(("parallel",), 48),
        name="proj_residual",
    )(*acts, w_bf16, x, g)
    return res if with_norm else (res[0], None)


def _xattn_prompt_kernel(q_ref, k_ref, v_ref, o_ref, *, scale):
    for h in range(q_ref.shape[2] // HEAD):
        sl = slice(h * HEAD, (h + 1) * HEAD)
        logits = _dot_nt(q_ref[0, :, sl], k_ref[0, :, sl].astype(BF16)) * scale
        p = jnp.exp(logits - jnp.max(logits, axis=-1, keepdims=True))
        attn = p / jnp.sum(p, axis=-1, keepdims=True)
        o_ref[0, :, sl] = _dot(attn.astype(BF16), v_ref[0, :, sl].astype(BF16)).astype(o_ref.dtype)


def _xattn_prompt(xq, mem_k, mem_v, b, s):
    width = xq.shape[1]
    m = mem_k.shape[1]
    tq = min(s, 512)
    out = pl.pallas_call(
        functools.partial(_xattn_prompt_kernel, scale=HEAD ** -0.5),
        grid=(b, s // tq),
        in_specs=[pl.BlockSpec((1, tq, width), lambda bi, i: (bi, i, 0)),
                  pl.BlockSpec((1, m, width), lambda bi, i: (bi, 0, 0)),
                  pl.BlockSpec((1, m, width), lambda bi, i: (bi, 0, 0))],
        out_specs=pl.BlockSpec((1, tq, width), lambda bi, i: (bi, i, 0)),
        out_shape=jax.ShapeDtypeStruct((b, s, width), BF16),
        compiler_params=_cparams(("parallel", "arbitrary"), 32),
        name="xattn_prompt",
    )(xq.reshape(b, s, width), mem_k, mem_v)
    return out.reshape(b * s, width)


def _xattn_sample_kernel(q_ref, k_ref, v_ref, o_ref, *, scale):
    q = q_ref[0]
    rows = q.shape[0]
    width = k_ref.shape[2]
    r_w = lax.broadcasted_iota(jnp.int32, (rows, width), 0)
    c_w = lax.broadcasted_iota(jnp.int32, (rows, width), 1)
    own_head = (r_w & 7) == (c_w >> 7)
    qbd = jnp.where(own_head, jnp.concatenate([q] * (width // HEAD), axis=1), 0.0).astype(BF16)
    logits = _dot_nt(qbd, k_ref[0].astype(BF16)) * scale
    p = jnp.exp(logits - jnp.max(logits, axis=-1, keepdims=True))
    attn = p / jnp.sum(p, axis=-1, keepdims=True)
    o_full = _dot(attn.astype(BF16), v_ref[0].astype(BF16))
    o_sel = jnp.where(own_head, o_full, 0.0)
    o_ref[0] = jnp.sum(o_sel.reshape(rows // 8, 8, width), axis=1)


def _xattn_sample(xq, mem_k, mem_v, b, s):
    width = xq.shape[1]
    nh = width // HEAD
    m = mem_k.shape[1]
    q4 = jnp.pad(xq.reshape(b, s, nh, HEAD), ((0, 0), (0, 0), (0, 8 - nh), (0, 0))).reshape(b, s * 8, HEAD)
    out = pl.pallas_call(
        functools.partial(_xattn_sample_kernel, scale=HEAD ** -0.5),
        grid=(b,),
        in_specs=[pl.BlockSpec((1, s * 8, HEAD), lambda bi: (bi, 0, 0)),
                  pl.BlockSpec((1, m, width), lambda bi: (bi, 0, 0)),
                  pl.BlockSpec((1, m, width), lambda bi: (bi, 0, 0))],
        out_specs=pl.BlockSpec((1, s, width), lambda bi: (bi, 0, 0)),
        out_shape=jax.ShapeDtypeStruct((b, s, width), F32),
        compiler_params=_cparams(("parallel",), 32),
        name="xattn_sample",
    )(q4, mem_k, mem_v)
    return out.reshape(b * s, width)


def _ffn_kernel(x_ref, g_ref, wg_ref, wu_ref, wd_ref, o_ref, h_ref):
    j = pl.program_id(1)

    @pl.when(j == 0)
    def _():
        x = x_ref[...]
        h_ref[...] = _rms(x, g_ref[...]).astype(h_ref.dtype)
        o_ref[...] = x

    h = h_ref[...]
    gate = _dot(h, wg_ref[...].astype(BF16))
    up = _dot(h, wu_ref[...].astype(BF16))
    act = (_silu(gate) * up).astype(BF16)
    o_ref[...] += _dot(act, wd_ref[...].astype(BF16))


def _ffn(x, gain, w_gate_up, w_down):
    t, d = x.shape
    f = w_down.shape[0]
    tm = min(t, 512)
    tf = 256
    assert f % tf == 0 and t % tm == 0
    nf = f // tf
    return pl.pallas_call(
        _ffn_kernel,
        grid=(t // tm, nf),
        in_specs=[pl.BlockSpec((tm, d), lambda i, j: (i, 0)),
                  pl.BlockSpec((1, d), lambda i, j: (0, 0)),
                  pl.BlockSpec((d, tf), lambda i, j: (0, j)),
                  pl.BlockSpec((d, tf), lambda i, j: (0, nf + j)),
                  pl.BlockSpec((tf, d), lambda i, j: (j, 0))],
        out_specs=pl.BlockSpec((tm, d), lambda i, j: (i, 0)),
        out_shape=jax.ShapeDtypeStruct((t, d), F32),
        scratch_shapes=[pltpu.VMEM((tm, d), BF16)],
        compiler_params=_cparams(("parallel", "arbitrary"), 48),
        name="ffn",
    )(x, gain.reshape(1, d), w_gate_up, w_gate_up, w_down)


def _decoder_layer(x, b, s, p, sb_attend, xattend, conv0, s0):
    d = x.shape[1]
    sbw = SB_HEADS * HEAD
    gw = GDN_HEADS * HEAD
    h = _rmsnorm_bf16(x, p["norm_mix_g"])
    q = _proj(h, p["w_in"], 0, sbw, p["sb_q_dtype"], p["sb_q_norm_g"])
    k_new = _proj(h, p["w_in"], sbw, sbw, F32, p["sb_k_norm_g"])
    v_new = _proj(h, p["w_in"], 2 * sbw, sbw, F32)
    raw = _proj(h, p["w_in"], 3 * sbw, 3 * gw, F32)
    gz = _proj(h, p["w_gz"], 0, gw, F32)
    ab = _proj(h, p["w_ab"], 0, HEAD, F32)
    o_sb = sb_attend(q, k_new, v_new)

    c = math.gcd(s, GDN_CHUNK)
    raw3, ab3, gz3 = raw.reshape(b, s, 3 * gw), ab.reshape(b, s, HEAD), gz.reshape(b, s, gw)
    conv_new = raw3[:, s - (GDN_CONV - 1):, :]
    n_valid = c
    if c < 8:
        padn = ((0, 0), (0, 8 - s), (0, 0))
        raw3, ab3, gz3 = jnp.pad(raw3, padn), jnp.pad(ab3, padn), jnp.pad(gz3, padn)
        c = 8
    o_gdn, state = _gdn(raw3, ab3, gz3, p["gdn_conv_w"], conv0, s0, p["gdn_a_log"], p["gdn_dt_bias"],
                        p["gdn_out_norm_g"], b, raw3.shape[1], c, n_valid)
    o_gdn = o_gdn[:, :s, :].reshape(b * s, gw)

    x1, hx = _proj_residual([o_sb, o_gdn], p["w_out_bf"], x, p["norm_x_g"])
    xq = _proj(hx, p["w_xq_bf"], 0, X_HEADS * HEAD, p["xq_dtype"], p["x_q_norm_g"])
    xo = xattend(xq)
    x2, _ = _proj_residual([xo], p["w_xo_bf"], x1)
    y = _ffn(x2, p["norm_ffn_g"], p["w_gate_up"], p["w_down"])
    return y, k_new, v_new, state, conv_new


def kernel(x_prompt, x_sample, mem_prompt, cache_sb_k, cache_sb_v, page_table, state_gdn, state_gdn_conv, cache_mem_k, cache_mem_v, norm_mix_g, w_in, sb_q_norm_g, sb_k_norm_g, sb_logit_bias, gdn_conv_w, gdn_a_log, gdn_dt_bias, gdn_out_norm_g, w_out, norm_x_g, norm_mem_g, w_xq, w_mk, w_mv, x_q_norm_g, x_k_norm_g, w_xo, norm_ffn_g, w_gate_up, w_down):
    bp, sp_, d = x_prompt.shape
    bs, ss, _ = x_sample.shape
    depth = w_in.shape[0]
    m = mem_prompt.shape[1]
    sbw = SB_HEADS * HEAD
    gw = GDN_HEADS * HEAD
    xw = X_HEADS * HEAD
    ch = 3 * gw
    ab0 = 3 * sbw + ch
    yp = x_prompt.reshape(bp * sp_, d)
    ys = x_sample.reshape(bs * ss, d)
    outs = [[] for _ in range(10)]
    for l in range(depth):
        p = {
            "norm_mix_g": norm_mix_g[l], "w_in": w_in[l], "sb_q_norm_g": sb_q_norm_g[l], "sb_k_norm_g": sb_k_norm_g[l],
            "gdn_conv_w": gdn_conv_w[l], "gdn_a_log": gdn_a_log[l], "gdn_dt_bias": gdn_dt_bias[l],
            "gdn_out_norm_g": gdn_out_norm_g[l], "norm_x_g": norm_x_g[l], "x_q_norm_g": x_q_norm_g[l],
            "norm_ffn_g": norm_ffn_g[l], "w_gate_up": w_gate_up[l], "w_down": w_down[l],
            "w_gz": w_in[l][:, ab0 + 2 * GDN_HEADS:],
            "w_ab": jnp.pad(w_in[l][:, ab0:ab0 + 2 * GDN_HEADS], ((0, 0), (0, HEAD - 2 * GDN_HEADS))),
            "w_out_bf": w_out[l].astype(BF16), "w_xq_bf": w_xq[l].astype(BF16), "w_xo_bf": w_xo[l].astype(BF16),
        }
        bias = sb_logit_bias[l]
        hm = _rmsnorm_bf16(mem_prompt.reshape(bp * m, d), norm_mem_g[l])
        mem_k = _proj(hm, w_mk[l], 0, xw, F32, x_k_norm_g[l]).reshape(bp, m, xw)
        mem_v = _proj(hm, w_mv[l], 0, xw, F32).reshape(bp, m, xw)

        pp = dict(p, sb_q_dtype=BF16, xq_dtype=BF16)
        yp, kp, vp, stp, cvp = _decoder_layer(
            yp, bp, sp_, pp,
            lambda q, k, v: _sb_prompt(q, k, v, bias, bp, sp_),
            lambda xq: _xattn_prompt(xq, mem_k, mem_v, bp, sp_),
            jnp.zeros((bp, GDN_CONV - 1, ch), F32), jnp.zeros((bp, GDN_HEADS, HEAD, HEAD), F32))

        ps = dict(p, sb_q_dtype=F32, xq_dtype=F32)
        cmk = cache_mem_k[l].reshape(bs, m, xw)
        cmv = cache_mem_v[l].reshape(bs, m, xw)
        ys, ks, vs, sts, cvs = _decoder_layer(
            ys, bs, ss, ps,
            lambda q, k, v: _sb_sample(q, k, v, bias, cache_sb_k[l], cache_sb_v[l], page_table, bs, ss),
            lambda xq: _xattn_sample(xq, cmk, cmv, bs, ss),
            state_gdn_conv[l], state_gdn[l])

        for acc, val in zip(outs, (
                kp.reshape(bp, sp_, SB_HEADS, HEAD), vp.reshape(bp, sp_, SB_HEADS, HEAD),
                ks.reshape(bs, ss, SB_HEADS, HEAD), vs.reshape(bs, ss, SB_HEADS, HEAD),
                stp.astype(state_gdn.dtype), cvp, sts.astype(state_gdn.dtype), cvs,
                mem_k.reshape(bp, m, X_HEADS, HEAD), mem_v.reshape(bp, m, X_HEADS, HEAD))):
            acc.append(val)
    return (yp.reshape(bp, sp_, d), ys.reshape(bs, ss, d)) + tuple(jnp.stack(o) for o in outs)
```

```python
import functools
import math

import jax
import jax.numpy as jnp
from jax import lax
from jax.experimental import pallas as pl
from jax.experimental.pallas import tpu as pltpu

F32 = jnp.float32
BF16 = jnp.bfloat16
EPS = 1e-6
HEAD = 128
SB_HEADS = 8
GDN_HEADS = 8
X_HEADS = 4
PAGE = 128
GDN_CONV = 4
GDN_CHUNK = 64
MIB = 1 << 20


def _cparams(sem, vmem_mib):
    return pltpu.CompilerParams(dimension_semantics=sem, vmem_limit_bytes=vmem_mib * MIB)


def _dot(a, b):
    return jnp.dot(a, b, preferred_element_type=F32)


def _dot_nt(a, b):
    return lax.dot_general(a, b, (((1,), (1,)), ((), ())), preferred_element_type=F32)


def _dot_tn(a, b):
    return lax.dot_general(a, b, (((0,), (0,)), ((), ())), preferred_element_type=F32)


def _split2(a):
    hi = a.astype(BF16)
    lo = (a - hi.astype(F32)).astype(BF16)
    return hi, lo


def _dot_hp(a, b, dot=_dot):
    ah, al = _split2(a)
    bh, bl = _split2(b)
    return dot(ah, bh) + (dot(ah, bl) + dot(al, bh))


def _dot_exact_lhs(m01, b):
    b0 = b.astype(BF16)
    r1 = b - b0.astype(F32)
    b1 = r1.astype(BF16)
    b2 = (r1 - b1.astype(F32)).astype(BF16)
    return _dot(m01, b0) + (_dot(m01, b1) + _dot(m01, b2))


def _dot_exact_rhs(a, m01):
    hi, lo = _split2(a)
    return _dot(hi, m01) + _dot(lo, m01)


def _softplus(z):
    return jnp.maximum(z, 0.0) + jnp.log1p(jnp.exp(-jnp.abs(z)))


def _silu(x):
    return x * jax.nn.sigmoid(x)


def _rms(y, g):
    ms = jnp.mean(y * y, axis=-1, keepdims=True)
    return y * lax.rsqrt(ms + EPS) * g


def _rmsnorm_kernel(x_ref, g_ref, o_ref):
    o_ref[...] = _rms(x_ref[...], g_ref[...]).astype(o_ref.dtype)


def _rmsnorm_bf16(x, g):
    t, d = x.shape
    tm = min(t, 512)
    return pl.pallas_call(
        _rmsnorm_kernel,
        grid=(t // tm,),
        in_specs=[pl.BlockSpec((tm, d), lambda i: (i, 0)), pl.BlockSpec((1, d), lambda i: (0, 0))],
        out_specs=pl.BlockSpec((tm, d), lambda i: (i, 0)),
        out_shape=jax.ShapeDtypeStruct((t, d), BF16),
        compiler_params=_cparams(("parallel",), 32),
        name="rmsnorm",
    )(x, g.reshape(1, d))


def _proj_kernel(x_ref, w_ref, g_ref, o_ref, *, head_norm):
    acc = _dot(x_ref[...], w_ref[...].astype(BF16))
    if head_norm:
        g = g_ref[...]
        for s in range(acc.shape[1] // HEAD):
            sl = slice(s * HEAD, (s + 1) * HEAD)
            o_ref[:, sl] = _rms(acc[:, sl], g).astype(o_ref.dtype)
    else:
        o_ref[...] = acc.astype(o_ref.dtype)


def _proj(h, w, col0, n_cols, out_dtype, head_gain=None, tn=512):
    t, k = h.shape
    tn = min(tn, n_cols)
    tm = min(t, 2048)
    assert col0 % tn == 0 and n_cols % tn == 0 and t % tm == 0
    cb = col0 // tn
    g = jnp.ones((1, HEAD), F32) if head_gain is None else head_gain.reshape(1, HEAD).astype(F32)
    return pl.pallas_call(
        functools.partial(_proj_kernel, head_norm=head_gain is not None),
        grid=(t // tm, n_cols // tn),
        in_specs=[pl.BlockSpec((tm, k), lambda i, j: (i, 0)),
                  pl.BlockSpec((k, tn), lambda i, j: (0, cb + j)),
                  pl.BlockSpec((1, HEAD), lambda i, j: (0, 0))],
        out_specs=pl.BlockSpec((tm, tn), lambda i, j: (i, j)),
        out_shape=jax.ShapeDtypeStruct((t, n_cols), out_dtype),
        compiler_params=_cparams(("parallel", "arbitrary"), 48),
        name="proj",
    )(h, w, g)


def _stick_logs(z):
    lse = jnp.log(1.0 + jnp.exp(-jnp.abs(z)))
    return -jnp.maximum(z, 0.0) - lse, jnp.minimum(z, 0.0) - lse


def _sb_prompt_kernel(bias_ref, q_ref, k_ref, v_ref, o_ref, *, tq, hpb, scale):
    hb = pl.program_id(1)
    i = pl.program_id(2)
    row = lax.broadcasted_iota(jnp.int32, (tq, tq), 0)
    col = lax.broadcasted_iota(jnp.int32, (tq, tq), 1)
    later = (row > col).astype(BF16)
    causal = col < row
    heads = range(hpb)
    qs = [q_ref[0, :, g * HEAD:(g + 1) * HEAD] for g in heads]
    biases = [bias_ref[hb * hpb + g] for g in heads]

    def block(j, carry, masked):
        start = pl.multiple_of(j * tq, tq)
        kb = k_ref[0, pl.ds(start, tq), :].astype(BF16)
        vb = v_ref[0, pl.ds(start, tq), :].astype(BF16)
        zs = [_dot_nt(qs[g], kb[:, g * HEAD:(g + 1) * HEAD]) * scale + biases[g] for g in heads]
        logs = [_stick_logs(z) for z in zs]
        stays = [jnp.where(causal, ls, 0.0) if masked else ls for ls, _ in logs]
        survs = [_dot_exact_rhs(st, later) for st in stays]
        out = []
        for g in heads:
            csum, acc = carry[g]
            w = jnp.exp(logs[g][1] + survs[g] + csum)
            if masked:
                w = jnp.where(causal, w, 0.0)
            acc = acc + _dot(w.astype(BF16), vb[:, g * HEAD:(g + 1) * HEAD])
            out.append((csum + jnp.sum(stays[g], axis=-1, keepdims=True), acc))
        return tuple(out)

    init = tuple((jnp.zeros((tq, 1), F32), jnp.zeros((tq, HEAD), F32)) for _ in heads)
    carry = block(i, init, True)
    carry = lax.fori_loop(0, i, lambda jj, c: block(i - 1 - jj, c, False), carry)
    for g in heads:
        o_ref[0, :, g * HEAD:(g + 1) * HEAD] = carry[g][1].astype(o_ref.dtype)


def _sb_prompt(q, k, v, bias, b, s):
    width = q.shape[1]
    nh = width // HEAD
    hpb = 4
    tq = min(s, 256)
    assert s % tq == 0 and nh % hpb == 0
    bw = hpb * HEAD
    q3, k3, v3 = (a.reshape(b, s, width) for a in (q, k, v))
    out = pl.pallas_call(
        functools.partial(_sb_prompt_kernel, tq=tq, hpb=hpb, scale=HEAD ** -0.5),
        grid=(b, nh // hpb, s // tq),
        in_specs=[pl.BlockSpec(memory_space=pltpu.SMEM),
                  pl.BlockSpec((1, tq, bw), lambda bi, h, i: (bi, i, h)),
                  pl.BlockSpec((1, s, bw), lambda bi, h, i: (bi, 0, h)),
                  pl.BlockSpec((1, s, bw), lambda bi, h, i: (bi, 0, h))],
        out_specs=pl.BlockSpec((1, tq, bw), lambda bi, h, i: (bi, i, h)),
        out_shape=jax.ShapeDtypeStruct((b, s, width), BF16),
        compiler_params=_cparams(("parallel", "parallel", "arbitrary"), 32),
        name="sb_prompt",
    )(bias.astype(F32), q3, k3, v3)
    return out.reshape(b * s, width)


def _sb_sample_kernel(pt_ref, bias_ref, q_ref, knew_ref, vnew_ref, *refs, n_pages, n_new, scale):
    del pt_ref
    k_pages = refs[:n_pages]
    v_pages = refs[n_pages:2 * n_pages]
    o_ref, kbf, vbf = refs[2 * n_pages:]
    nh = kbf.shape[0]
    past = n_pages * PAGE
    total = past + PAGE
    qrows = q_ref.shape[2]
    new_rows = knew_ref.shape[2]
    for h in range(nh):
        for p in range(n_pages):
            kbf[h, p * PAGE:(p + 1) * PAGE, :] = k_pages[p][0, 0, pl.ds(h, PAGE, stride=nh), :].astype(BF16)
            vbf[h, p * PAGE:(p + 1) * PAGE, :] = v_pages[p][0, 0, pl.ds(h, PAGE, stride=nh), :].astype(BF16)
        kbf[h, past:total, :] = jnp.zeros((PAGE, HEAD), BF16)
        vbf[h, past:total, :] = jnp.zeros((PAGE, HEAD), BF16)
        kbf[h, past:past + new_rows, :] = knew_ref[0, h].astype(BF16)
        vbf[h, past:past + new_rows, :] = vnew_ref[0, h].astype(BF16)

    z = jnp.concatenate(
        [_dot_nt(q_ref[0, h].astype(BF16), kbf[h]) * scale + bias_ref[h] for h in range(nh)], axis=0)
    rows = nh * qrows
    log_stay, log_break = _stick_logs(z)
    r_k = lax.broadcasted_iota(jnp.int32, (rows, total), 0)
    c_k = lax.broadcasted_iota(jnp.int32, (rows, total), 1)
    t_of_row = r_k & (qrows - 1)
    valid = (c_k < past) | (((c_k - past) < t_of_row) & ((c_k - past) < n_new))
    log_stay = jnp.where(valid, log_stay, 0.0)

    rr = lax.broadcasted_iota(jnp.int32, (PAGE, PAGE), 0)
    cc = lax.broadcasted_iota(jnp.int32, (PAGE, PAGE), 1)
    later = (rr > cc).astype(BF16)
    csum = jnp.zeros((rows, 1), F32)
    pieces = []
    for blk in reversed(range(total // PAGE)):
        lsb = log_stay[:, blk * PAGE:(blk + 1) * PAGE]
        pieces.append(_dot_exact_rhs(lsb, later) + csum)
        csum = csum + jnp.sum(lsb, axis=-1, keepdims=True)
    survive = jnp.concatenate(pieces[::-1], axis=1)
    w = jnp.where(valid, jnp.exp(log_break + survive), 0.0)
    for h in range(nh):
        o_h = _dot(w[h * qrows:(h + 1) * qrows, :].astype(BF16), vbf[h])
        o_ref[0, :, h * HEAD:(h + 1) * HEAD] = o_h[0:n_new, :]


def _sb_sample(q, k_new, v_new, bias, cache_k, cache_v, layer, page_table, b, s):
    width = q.shape[1]
    nh = width // HEAD
    n_pages = page_table.shape[1]
    qrows, new_rows = 8, 16
    assert s <= qrows and cache_k.shape[2] == PAGE

    def per_head(a, rows):
        a = a.reshape(b, s, nh, HEAD).transpose(0, 2, 1, 3)
        return jnp.pad(a, ((0, 0), (0, 0), (0, rows - s), (0, 0)))

    def page_spec(p):
        return pl.BlockSpec((1, 1, PAGE * nh, HEAD), lambda bi, pt: (layer, pt[bi, p], 0, 0))

    flat = lambda c: c.reshape(c.shape[0], c.shape[1], PAGE * nh, HEAD)

    grid_spec = pltpu.PrefetchScalarGridSpec(
        num_scalar_prefetch=1,
        grid=(b,),
        in_specs=[pl.BlockSpec(memory_space=pltpu.SMEM),
                  pl.BlockSpec((1, nh, qrows, HEAD), lambda bi, pt: (bi, 0, 0, 0)),
                  pl.BlockSpec((1, nh, new_rows, HEAD), lambda bi, pt: (bi, 0, 0, 0)),
                  pl.BlockSpec((1, nh, new_rows, HEAD), lambda bi, pt: (bi, 0, 0, 0))]
        + [page_spec(p) for p in range(n_pages)] * 2,
        out_specs=pl.BlockSpec((1, s, width), lambda bi, pt: (bi, 0, 0)),
        scratch_shapes=[pltpu.VMEM((nh, n_pages * PAGE + PAGE, HEAD), BF16),
                        pltpu.VMEM((nh, n_pages * PAGE + PAGE, HEAD), BF16)],
    )
    out = pl.pallas_call(
        functools.partial(_sb_sample_kernel, n_pages=n_pages, n_new=s, scale=HEAD ** -0.5),
        grid_spec=grid_spec,
        out_shape=jax.ShapeDtypeStruct((b, s, width), F32),
        compiler_params=_cparams(("arbitrary",), 56),
        name="sb_sample",
    )(page_table, bias.astype(F32), per_head(q, qrows), per_head(k_new, new_rows), per_head(v_new, new_rows),
      *([flat(cache_k)] * n_pages), *([flat(cache_v)] * n_pages))
    return out.reshape(b * s, width)


def _gdn_kernel(raw_ref, ab_ref, gz_ref, convw_ref, conv0_ref, s0_ref, alog_ref, dtb_ref, gn_ref,
                o_ref, s_ref, pad_ref, *, c, n_valid):
    n = pl.program_id(1)
    nh = s_ref.shape[1]
    qkw = nh * HEAD

    @pl.when(n == 0)
    def _():
        pad_ref[0:8, :] = jnp.zeros((8, pad_ref.shape[1]), F32)
        pad_ref[5:8, :] = conv0_ref[0]
        s_ref[...] = s0_ref[...]

    pad_ref[8:8 + c, :] = raw_ref[0]
    cw = convw_ref[...]
    conv = pad_ref[5:5 + c, :] * cw[0:1, :]
    for j in range(1, GDN_CONV):
        conv = conv + pad_ref[5 + j:5 + j + c, :] * cw[j:j + 1, :]
    pad_ref[5:8, :] = pad_ref[5 + c:8 + c, :]
    xs = _silu(conv)

    ab = ab_ref[0]
    live = lax.broadcasted_iota(jnp.int32, ab.shape, 0) < n_valid
    g_all = jnp.where(live, -jnp.exp(alog_ref[...]) * _softplus(ab + dtb_ref[...]), 0.0)
    beta_all = jnp.where(live, jax.nn.sigmoid(ab), 0.0)
    rc = lax.broadcasted_iota(jnp.int32, (c, c), 0)
    cc = lax.broadcasted_iota(jnp.int32, (c, c), 1)
    cum_all = _dot_exact_lhs((cc <= rc).astype(BF16), g_all)
    cum_last = cum_all[c - 1:c, :]
    ecum_all = jnp.exp(cum_all)
    eend_all = jnp.exp(cum_last - cum_all)
    glast_all = jnp.exp(cum_last)
    gn = gn_ref[...]

    hpg = min(nh, max(1, GDN_CHUNK // c))
    rows = hpg * c
    groups = [list(range(g0, g0 + hpg)) for g0 in range(0, nh, hpg)]
    ri = lax.broadcasted_iota(jnp.int32, (rows, rows), 0)
    ci = lax.broadcasted_iota(jnp.int32, (rows, rows), 1)
    shift = int(math.log2(c))
    same = (ri >> shift) == (ci >> shift)
    strict = same & (ri > ci)
    incl = same & (ri >= ci)
    upto = incl.astype(BF16)
    eye = (ri == ci).astype(F32)

    def stack(pieces):
        return pieces[0] if len(pieces) == 1 else jnp.concatenate(pieces, axis=0)

    def unit(x):
        return x * lax.rsqrt(jnp.sum(x * x, axis=-1, keepdims=True) + EPS)

    ops = []
    for hs in groups:
        q = unit(stack([xs[:, h * HEAD:(h + 1) * HEAD] for h in hs])) * (HEAD ** -0.5)
        k = unit(stack([xs[:, qkw + h * HEAD:qkw + (h + 1) * HEAD] for h in hs]))
        v = stack([xs[:, 2 * qkw + h * HEAD:2 * qkw + (h + 1) * HEAD] for h in hs])
        col = lambda a, off: stack([a[:, off + h:off + h + 1] for h in hs])
        ops.append(dict(q=q, k=k, v=v, g=col(g_all, 0), beta=col(beta_all, nh),
                        ecum=col(ecum_all, 0), eend=col(eend_all, 0)))
    diffs = [_dot_exact_lhs(upto, jnp.where(strict, jnp.broadcast_to(o["g"], (rows, rows)), 0.0)) for o in ops]
    kks = [_dot_hp(o["k"], o["k"], _dot_nt) for o in ops]
    qks = [_dot_hp(o["q"], o["k"], _dot_nt) for o in ops]
    decays = [jnp.where(incl, jnp.exp(d), 0.0) for d in diffs]
    npows = [jnp.where(strict, o["beta"] * kk * dec, 0.0) for o, kk, dec in zip(ops, kks, decays)]
    invs = [eye - n for n in npows]
    for _ in range(shift - 1):
        npows = [_dot_hp(n, n) for n in npows]
        invs = [inv + _dot_hp(inv, n) for inv, n in zip(invs, npows)]
    sols = [_dot_hp(inv, jnp.concatenate([o["v"] * o["beta"], o["k"] * (o["beta"] * o["ecum"])], axis=-1))
            for inv, o in zip(invs, ops)]
    qdecs = [o["q"] * o["ecum"] for o in ops]
    from_state = []
    for hs, sol, qdec in zip(groups, sols, qdecs):
        for j, h in enumerate(hs):
            r = slice(j * c, (j + 1) * c)
            from_state.append(_dot_hp(jnp.concatenate([sol[r, HEAD:], qdec[r, :]], axis=0), s_ref[0, h]))
    v_news, outs = [], []
    for gi, (hs, sol, qk, dec) in enumerate(zip(groups, sols, qks, decays)):
        mine = from_state[gi * hpg:(gi + 1) * hpg]
        v_new = sol[:, :HEAD] - stack([m[:c, :] for m in mine])
        v_news.append(v_new)
        outs.append(stack([m[c:, :] for m in mine]) + _dot_hp(qk * dec, v_new))
    for hs, o, v_new in zip(groups, ops, v_news):
        k_end = o["k"] * o["eend"]
        for j, h in enumerate(hs):
            r = slice(j * c, (j + 1) * c)
            s_ref[0, h] = glast_all[:, h:h + 1] * s_ref[0, h] + _dot_hp(k_end[r, :], v_new[r, :], _dot_tn)
    for hs, out in zip(groups, outs):
        normed = _rms(out, gn)
        for j, h in enumerate(hs):
            sl = slice(h * HEAD, (h + 1) * HEAD)
            o_ref[0, :, sl] = (normed[j * c:(j + 1) * c, :] * _silu(gz_ref[0, :, sl])).astype(o_ref.dtype)


def _gdn(raw, ab, gz, conv_w, conv0, s0, a_log, dt_bias, out_gain, b, s, c, n_valid):
    nh = s0.shape[1]
    ch = raw.shape[2]
    pad_lane = lambda a: jnp.pad(a.astype(F32), (0, HEAD - a.shape[0])).reshape(1, HEAD)
    out, state = pl.pallas_call(
        functools.partial(_gdn_kernel, c=c, n_valid=n_valid),
        grid=(b, s // c),
        in_specs=[pl.BlockSpec((1, c, ch), lambda bi, n: (bi, n, 0)),
                  pl.BlockSpec((1, c, HEAD), lambda bi, n: (bi, n, 0)),
                  pl.BlockSpec((1, c, nh * HEAD), lambda bi, n: (bi, n, 0)),
                  pl.BlockSpec((GDN_CONV, ch), lambda bi, n: (0, 0)),
                  pl.BlockSpec((1, GDN_CONV - 1, ch), lambda bi, n: (bi, 0, 0)),
                  pl.BlockSpec((1, nh, HEAD, HEAD), lambda bi, n: (bi, 0, 0, 0)),
                  pl.BlockSpec((1, HEAD), lambda bi, n: (0, 0)),
                  pl.BlockSpec((1, HEAD), lambda bi, n: (0, 0)),
                  pl.BlockSpec((1, HEAD), lambda bi, n: (0, 0))],
        out_specs=[pl.BlockSpec((1, c, nh * HEAD), lambda bi, n: (bi, n, 0)),
                   pl.BlockSpec((1, nh, HEAD, HEAD), lambda bi, n: (bi, 0, 0, 0))],
        out_shape=[jax.ShapeDtypeStruct((b, s, nh * HEAD), BF16),
                   jax.ShapeDtypeStruct(s0.shape, F32)],
        scratch_shapes=[pltpu.VMEM((8 + c, ch), F32)],
        compiler_params=_cparams(("parallel", "arbitrary"), 32),
        name="gdn",
    )(raw, ab, gz, conv_w, conv0, s0, pad_lane(a_log), pad_lane(dt_bias), out_gain.reshape(1, HEAD))
    return out, state


def _out_kernel(*refs, n_a, with_norm):
    a_refs = refs[:n_a]
    w_ref, x_ref, g_ref = refs[n_a:n_a + 3]
    outs = refs[n_a + 3:]
    acc = x_ref[...]
    off = 0
    for a in a_refs:
        ka = a.shape[1]
        acc = acc + _dot(a[...].astype(BF16), w_ref[off:off + ka, :])
        off += ka
    outs[0][...] = acc
    if with_norm:
        outs[1][...] = _rms(acc, g_ref[...]).astype(outs[1].dtype)


def _proj_residual(acts, w_bf16, x, next_gain=None):
    t, d = x.shape
    tm = min(t, 256)
    with_norm = next_gain is not None
    g = (next_gain if with_norm else jnp.ones((d,), F32)).reshape(1, d)
    out_shape = [jax.ShapeDtypeStruct((t, d), F32)]
    out_specs = [pl.BlockSpec((tm, d), lambda i: (i, 0))]
    if with_norm:
        out_shape.append(jax.ShapeDtypeStruct((t, d), BF16))
        out_specs.append(pl.BlockSpec((tm, d), lambda i: (i, 0)))
    res = pl.pallas_call(
        functools.partial(_out_kernel, n_a=len(acts), with_norm=with_norm),
        grid=(t // tm,),
        in_specs=[pl.BlockSpec((tm, a.shape[1]), lambda i: (i, 0)) for a in acts]
        + [pl.BlockSpec(w_bf16.shape, lambda i: (0, 0)),
           pl.BlockSpec((tm, d), lambda i: (i, 0)),
           pl.BlockSpec((1, d), lambda i: (0, 0))],
        out_specs=out_specs,
        out_shape=out_shape,
        compiler_params=_cparams(("parallel",), 48),
        name="proj_residual",
    )(*acts, w_bf16, x, g)
    return res if with_norm else (res[0], None)


def _xattn_prompt_kernel(q_ref, k_ref, v_ref, o_ref, *, scale):
    for h in range(q_ref.shape[2] // HEAD):
        sl = slice(h * HEAD, (h + 1) * HEAD)
        logits = _dot_nt(q_ref[0, :, sl], k_ref[0, :, sl].astype(BF16)) * scale
        p = jnp.exp(logits - jnp.max(logits, axis=-1, keepdims=True))
        attn = p / jnp.sum(p, axis=-1, keepdims=True)
        o_ref[0, :, sl] = _dot(attn.astype(BF16), v_ref[0, :, sl].astype(BF16)).astype(o_ref.dtype)


def _xattn_prompt(xq, mem_k, mem_v, b, s):
    width = xq.shape[1]
    m = mem_k.shape[1]
    tq = min(s, 512)
    out = pl.pallas_call(
        functools.partial(_xattn_prompt_kernel, scale=HEAD ** -0.5),
        grid=(b, s // tq),
        in_specs=[pl.BlockSpec((1, tq, width), lambda bi, i: (bi, i, 0)),
                  pl.BlockSpec((1, m, width), lambda bi, i: (bi, 0, 0)),
                  pl.BlockSpec((1, m, width), lambda bi, i: (bi, 0, 0))],
        out_specs=pl.BlockSpec((1, tq, width), lambda bi, i: (bi, i, 0)),
        out_shape=jax.ShapeDtypeStruct((b, s, width), BF16),
        compiler_params=_cparams(("parallel", "arbitrary"), 32),
        name="xattn_prompt",
    )(xq.reshape(b, s, width), mem_k, mem_v)
    return out.reshape(b * s, width)


def _xattn_sample_kernel(q_ref, k_ref, v_ref, o_ref, *, n_new, scale):
    nh = q_ref.shape[1]
    m = k_ref.shape[2] // nh
    heads = range(nh)
    logits = [_dot_nt(q_ref[0, h].astype(BF16), k_ref[0, 0, pl.ds(h, m, stride=nh), :].astype(BF16)) * scale
              for h in heads]
    ps = [jnp.exp(lg - jnp.max(lg, axis=-1, keepdims=True)) for lg in logits]
    attns = [(p / jnp.sum(p, axis=-1, keepdims=True)).astype(BF16) for p in ps]
    for h in heads:
        o_h = _dot(attns[h], v_ref[0, 0, pl.ds(h, m, stride=nh), :].astype(BF16))
        o_ref[0, :, h * HEAD:(h + 1) * HEAD] = o_h[0:n_new, :]


def _xattn_sample(xq, mem_k, mem_v, layer, b, s):
    width = xq.shape[1]
    nh = width // HEAD
    m = mem_k.shape[2]
    qrows = 8
    q4 = jnp.pad(xq.reshape(b, s, nh, HEAD).transpose(0, 2, 1, 3), ((0, 0), (0, 0), (0, qrows - s), (0, 0)))
    mem_spec = pl.BlockSpec((1, 1, m * nh, HEAD), lambda bi: (layer, bi, 0, 0))
    flat = lambda c: c.reshape(c.shape[0], c.shape[1], m * nh, HEAD)
    out = pl.pallas_call(
        functools.partial(_xattn_sample_kernel, n_new=s, scale=HEAD ** -0.5),
        grid=(b,),
        in_specs=[pl.BlockSpec((1, nh, qrows, HEAD), lambda bi: (bi, 0, 0, 0)), mem_spec, mem_spec],
        out_specs=pl.BlockSpec((1, s, width), lambda bi: (bi, 0, 0)),
        out_shape=jax.ShapeDtypeStruct((b, s, width), F32),
        compiler_params=_cparams(("parallel",), 32),
        name="xattn_sample",
    )(q4, flat(mem_k), flat(mem_v))
    return out.reshape(b * s, width)


def _ffn_kernel(x_ref, g_ref, wg_ref, wu_ref, wd_ref, o_ref, h_ref):
    j = pl.program_id(1)

    @pl.when(j == 0)
    def _():
        x = x_ref[...]
        h_ref[...] = _rms(x, g_ref[...]).astype(h_ref.dtype)
        o_ref[...] = x

    h = h_ref[...]
    gate = _dot(h, wg_ref[...].astype(BF16))
    up = _dot(h, wu_ref[...].astype(BF16))
    act = (_silu(gate) * up).astype(BF16)
    o_ref[...] += _dot(act, wd_ref[...].astype(BF16))


def _ffn(x, gain, w_gate_up, w_down):
    t, d = x.shape
    f = w_down.shape[0]
    tm = min(t, 1024)
    tf = 256
    assert f % tf == 0 and t % tm == 0
    nf = f // tf
    return pl.pallas_call(
        _ffn_kernel,
        grid=(t // tm, nf),
        in_specs=[pl.BlockSpec((tm, d), lambda i, j: (i, 0)),
                  pl.BlockSpec((1, d), lambda i, j: (0, 0)),
                  pl.BlockSpec((d, tf), lambda i, j: (0, j)),
                  pl.BlockSpec((d, tf), lambda i, j: (0, nf + j)),
                  pl.BlockSpec((tf, d), lambda i, j: (j, 0))],
        out_specs=pl.BlockSpec((tm, d), lambda i, j: (i, 0)),
        out_shape=jax.ShapeDtypeStruct((t, d), F32),
        scratch_shapes=[pltpu.VMEM((tm, d), BF16)],
        compiler_params=_cparams(("parallel", "arbitrary"), 60),
        name="ffn",
    )(x, gain.reshape(1, d), w_gate_up, w_gate_up, w_down)


def _decoder_layer(x, b, s, p, sb_attend, xattend, conv0, s0):
    d = x.shape[1]
    sbw = SB_HEADS * HEAD
    gw = GDN_HEADS * HEAD
    h = _rmsnorm_bf16(x, p["norm_mix_g"])
    q = _proj(h, p["w_in"], 0, sbw, p["sb_q_dtype"], p["sb_q_norm_g"])
    k_new = _proj(h, p["w_in"], sbw, sbw, F32, p["sb_k_norm_g"])
    v_new = _proj(h, p["w_in"], 2 * sbw, sbw, F32)
    raw = _proj(h, p["w_in"], 3 * sbw, 3 * gw, F32)
    gz = _proj(h, p["w_gz"], 0, gw, F32)
    ab = _proj(h, p["w_ab"], 0, HEAD, F32)
    o_sb = sb_attend(q, k_new, v_new)

    c = math.gcd(s, GDN_CHUNK)
    raw3, ab3, gz3 = raw.reshape(b, s, 3 * gw), ab.reshape(b, s, HEAD), gz.reshape(b, s, gw)
    conv_new = raw3[:, s - (GDN_CONV - 1):, :]
    n_valid = c
    if c < 8:
        padn = ((0, 0), (0, 8 - s), (0, 0))
        raw3, ab3, gz3 = jnp.pad(raw3, padn), jnp.pad(ab3, padn), jnp.pad(gz3, padn)
        c = 8
    o_gdn, state = _gdn(raw3, ab3, gz3, p["gdn_conv_w"], conv0, s0, p["gdn_a_log"], p["gdn_dt_bias"],
                        p["gdn_out_norm_g"], b, raw3.shape[1], c, n_valid)
    o_gdn = o_gdn[:, :s, :].reshape(b * s, gw)

    x1, hx = _proj_residual([o_sb, o_gdn], p["w_out_bf"], x, p["norm_x_g"])
    xq = _proj(hx, p["w_xq_bf"], 0, X_HEADS * HEAD, p["xq_dtype"], p["x_q_norm_g"])
    xo = xattend(xq)
    x2, _ = _proj_residual([xo], p["w_xo_bf"], x1)
    y = _ffn(x2, p["norm_ffn_g"], p["w_gate_up"], p["w_down"])
    return y, k_new, v_new, state, conv_new


def kernel(x_prompt, x_sample, mem_prompt, cache_sb_k, cache_sb_v, page_table, state_gdn, state_gdn_conv, cache_mem_k, cache_mem_v, norm_mix_g, w_in, sb_q_norm_g, sb_k_norm_g, sb_logit_bias, gdn_conv_w, gdn_a_log, gdn_dt_bias, gdn_out_norm_g, w_out, norm_x_g, norm_mem_g, w_xq, w_mk, w_mv, x_q_norm_g, x_k_norm_g, w_xo, norm_ffn_g, w_gate_up, w_down):
    bp, sp_, d = x_prompt.shape
    bs, ss, _ = x_sample.shape
    depth = w_in.shape[0]
    m = mem_prompt.shape[1]
    sbw = SB_HEADS * HEAD
    gw = GDN_HEADS * HEAD
    xw = X_HEADS * HEAD
    ch = 3 * gw
    ab0 = 3 * sbw + ch
    yp = x_prompt.reshape(bp * sp_, d)
    ys = x_sample.reshape(bs * ss, d)
    outs = [[] for _ in range(10)]
    for l in range(depth):
        p = {
            "norm_mix_g": norm_mix_g[l], "w_in": w_in[l], "sb_q_norm_g": sb_q_norm_g[l], "sb_k_norm_g": sb_k_norm_g[l],
            "gdn_conv_w": gdn_conv_w[l], "gdn_a_log": gdn_a_log[l], "gdn_dt_bias": gdn_dt_bias[l],
            "gdn_out_norm_g": gdn_out_norm_g[l], "norm_x_g": norm_x_g[l], "x_q_norm_g": x_q_norm_g[l],
            "norm_ffn_g": norm_ffn_g[l], "w_gate_up": w_gate_up[l], "w_down": w_down[l],
            "w_gz": w_in[l][:, ab0 + 2 * GDN_HEADS:],
            "w_ab": jnp.pad(w_in[l][:, ab0:ab0 + 2 * GDN_HEADS], ((0, 0), (0, HEAD - 2 * GDN_HEADS))),
            "w_out_bf": w_out[l].astype(BF16), "w_xq_bf": w_xq[l].astype(BF16), "w_xo_bf": w_xo[l].astype(BF16),
        }
        bias = sb_logit_bias[l]
        hm = _rmsnorm_bf16(mem_prompt.reshape(bp * m, d), norm_mem_g[l])
        mem_k = _proj(hm, w_mk[l], 0, xw, F32, x_k_norm_g[l]).reshape(bp, m, xw)
        mem_v = _proj(hm, w_mv[l], 0, xw, F32).reshape(bp, m, xw)

        pp = dict(p, sb_q_dtype=BF16, xq_dtype=BF16)
        yp, kp, vp, stp, cvp = _decoder_layer(
            yp, bp, sp_, pp,
            lambda q, k, v: _sb_prompt(q, k, v, bias, bp, sp_),
            lambda xq: _xattn_prompt(xq, mem_k, mem_v, bp, sp_),
            jnp.zeros((bp, GDN_CONV - 1, ch), F32), jnp.zeros((bp, GDN_HEADS, HEAD, HEAD), F32))

        ps = dict(p, sb_q_dtype=F32, xq_dtype=F32)
        ys, ks, vs, sts, cvs = _decoder_layer(
            ys, bs, ss, ps,
            lambda q, k, v: _sb_sample(q, k, v, bias, cache_sb_k, cache_sb_v, l, page_table, bs, ss),
            lambda xq: _xattn_sample(xq, cache_mem_k, cache_mem_v, l, bs, ss),
            state_gdn_conv[l], state_gdn[l])

        for acc, val in zip(outs, (
                kp.reshape(bp, sp_, SB_HEADS, HEAD), vp.reshape(bp, sp_, SB_HEADS, HEAD),
                ks.reshape(bs, ss, SB_HEADS, HEAD), vs.reshape(bs, ss, SB_HEADS, HEAD),
                stp.astype(state_gdn.dtype), cvp, sts.astype(state_gdn.dtype), cvs,
                mem_k.reshape(bp, m, X_HEADS, HEAD), mem_v.reshape(bp, m, X_HEADS, HEAD))):
            acc.append(val)
    return (yp.reshape(bp, sp_, d), ys.reshape(bs, ss, d)) + tuple(jnp.stack(o) for o in outs)
```

```python
import functools
import math

import jax
import jax.numpy as jnp
from jax import lax
from jax.experimental import pallas as pl
from jax.experimental.pallas import tpu as pltpu

F32 = jnp.float32
BF16 = jnp.bfloat16
EPS = 1e-6
HEAD = 128
SB_HEADS = 8
GDN_HEADS = 8
X_HEADS = 4
PAGE = 128
GDN_CONV = 4
GDN_CHUNK = 64
MIB = 1 << 20


def _cparams(sem, vmem_mib):
    return pltpu.CompilerParams(dimension_semantics=sem, vmem_limit_bytes=vmem_mib * MIB)


def _dot(a, b):
    return jnp.dot(a, b, preferred_element_type=F32)


def _dot_nt(a, b):
    return lax.dot_general(a, b, (((1,), (1,)), ((), ())), preferred_element_type=F32)


def _dot_tn(a, b):
    return lax.dot_general(a, b, (((0,), (0,)), ((), ())), preferred_element_type=F32)


def _split2(a):
    hi = a.astype(BF16)
    lo = (a - hi.astype(F32)).astype(BF16)
    return hi, lo


def _dot_hp(a, b, dot=_dot):
    ah, al = _split2(a)
    bh, bl = _split2(b)
    return dot(ah, bh) + (dot(ah, bl) + dot(al, bh))


def _dot_bf(a, b, dot=_dot):
    return dot(a.astype(BF16), b.astype(BF16))


def _dot_exact_lhs(m01, b, terms=3):
    acc = None
    for _ in range(terms):
        piece = b.astype(BF16)
        b = b - piece.astype(F32)
        part = _dot(m01, piece)
        acc = part if acc is None else acc + part
    return acc


def _dot_exact_rhs(a, m_twice):
    hi, lo = _split2(a)
    return _dot(jnp.concatenate([hi, lo], axis=1), m_twice)


def _softplus(z):
    return jnp.maximum(z, 0.0) + jnp.log1p(jnp.exp(-jnp.abs(z)))


def _softplus_fast(z):
    return jnp.maximum(z, 0.0) + jnp.log(1.0 + jnp.exp(-jnp.abs(z)))


def _silu(x):
    return x * jax.nn.sigmoid(x)


def _rms(y, g):
    ms = jnp.mean(y * y, axis=-1, keepdims=True)
    return y * lax.rsqrt(ms + EPS) * g


def _rmsnorm_kernel(x_ref, g_ref, o_ref):
    o_ref[...] = _rms(x_ref[...], g_ref[...]).astype(o_ref.dtype)


def _rmsnorm_bf16(x, g):
    t, d = x.shape
    tm = min(t, 512)
    return pl.pallas_call(
        _rmsnorm_kernel,
        grid=(t // tm,),
        in_specs=[pl.BlockSpec((tm, d), lambda i: (i, 0)), pl.BlockSpec((1, d), lambda i: (0, 0))],
        out_specs=pl.BlockSpec((tm, d), lambda i: (i, 0)),
        out_shape=jax.ShapeDtypeStruct((t, d), BF16),
        compiler_params=_cparams(("parallel",), 32),
        name="rmsnorm",
    )(x, g.reshape(1, d))


def _proj_kernel(x_ref, w_ref, g_ref, o_ref, *, head_norm):
    acc = _dot(x_ref[...], w_ref[...].astype(BF16))
    if head_norm:
        g = g_ref[...]
        for s in range(acc.shape[1] // HEAD):
            sl = slice(s * HEAD, (s + 1) * HEAD)
            o_ref[:, sl] = _rms(acc[:, sl], g).astype(o_ref.dtype)
    else:
        o_ref[...] = acc.astype(o_ref.dtype)


def _proj(h, w, col0, n_cols, out_dtype, head_gain=None):
    t, k = h.shape
    tm = min(t, 4096)
    tn = min(256 if tm == 4096 else 512, n_cols)
    assert col0 % tn == 0 and n_cols % tn == 0 and t % tm == 0
    cb = col0 // tn
    g = jnp.ones((1, HEAD), F32) if head_gain is None else head_gain.reshape(1, HEAD).astype(F32)
    return pl.pallas_call(
        functools.partial(_proj_kernel, head_norm=head_gain is not None),
        grid=(t // tm, n_cols // tn),
        in_specs=[pl.BlockSpec((tm, k), lambda i, j: (i, 0), pipeline_mode=pl.Buffered(1)),
                  pl.BlockSpec((k, tn), lambda i, j: (0, cb + j)),
                  pl.BlockSpec((1, HEAD), lambda i, j: (0, 0))],
        out_specs=pl.BlockSpec((tm, tn), lambda i, j: (i, j)),
        out_shape=jax.ShapeDtypeStruct((t, n_cols), out_dtype),
        compiler_params=_cparams(("parallel", "arbitrary"), 56),
        name="proj",
    )(h, w, g)


def _minus_later_twice(n):
    row = lax.broadcasted_iota(jnp.int32, (n, n), 0)
    col = lax.broadcasted_iota(jnp.int32, (n, n), 1)
    m = jnp.where(row > col, -1.0, 0.0).astype(BF16)
    return jnp.concatenate([m, m], axis=0)


def _sb_prompt_kernel(bias_ref, q_ref, k_ref, v_ref, o_ref, *, tq, hpb):
    hb = pl.program_id(1)
    i = pl.program_id(2)
    row = lax.broadcasted_iota(jnp.int32, (tq, tq), 0)
    col = lax.broadcasted_iota(jnp.int32, (tq, tq), 1)
    minus_later = _minus_later_twice(tq)
    causal = col < row
    heads = range(hpb)
    qs = [q_ref[0, :, g * HEAD:(g + 1) * HEAD] for g in heads]
    biases = [bias_ref[hb * hpb + g] for g in heads]

    def block(j, carry, masked):
        start = pl.multiple_of(j * tq, tq)
        kb = k_ref[0, pl.ds(start, tq), :].astype(BF16)
        vb = v_ref[0, pl.ds(start, tq), :].astype(BF16)
        zs = [_dot_nt(qs[g], kb[:, g * HEAD:(g + 1) * HEAD]) + biases[g] for g in heads]
        sps = [_softplus_fast(z) for z in zs]
        leaves = [jnp.where(causal, sp, 0.0) if masked else sp for sp in sps]
        survs = [_dot_exact_rhs(lv, minus_later) for lv in leaves]
        out = []
        for g in heads:
            csum, acc = carry[g]
            w = jnp.exp(zs[g] - sps[g] + survs[g] + csum)
            if masked:
                w = jnp.where(causal, w, 0.0)
            acc = acc + _dot(w.astype(BF16), vb[:, g * HEAD:(g + 1) * HEAD])
            out.append((csum - jnp.sum(leaves[g], axis=-1, keepdims=True), acc))
        return tuple(out)

    init = tuple((jnp.zeros((tq, 1), F32), jnp.zeros((tq, HEAD), F32)) for _ in heads)
    carry = block(i, init, True)
    carry = lax.fori_loop(0, i, lambda jj, c: block(i - 1 - jj, c, False), carry)
    for g in heads:
        o_ref[0, :, g * HEAD:(g + 1) * HEAD] = carry[g][1].astype(o_ref.dtype)


def _sb_prompt(q, k, v, bias, b, s):
    width = q.shape[1]
    nh = width // HEAD
    hpb = 4
    tq = min(s, 256)
    assert s % tq == 0 and nh % hpb == 0
    bw = hpb * HEAD
    q3, k3, v3 = (a.reshape(b, s, width) for a in (q, k, v))
    out = pl.pallas_call(
        functools.partial(_sb_prompt_kernel, tq=tq, hpb=hpb),
        grid=(b, nh // hpb, s // tq),
        in_specs=[pl.BlockSpec(memory_space=pltpu.SMEM),
                  pl.BlockSpec((1, tq, bw), lambda bi, h, i: (bi, i, h)),
                  pl.BlockSpec((1, s, bw), lambda bi, h, i: (bi, 0, h)),
                  pl.BlockSpec((1, s, bw), lambda bi, h, i: (bi, 0, h))],
        out_specs=pl.BlockSpec((1, tq, bw), lambda bi, h, i: (bi, i, h)),
        out_shape=jax.ShapeDtypeStruct((b, s, width), BF16),
        compiler_params=_cparams(("parallel", "parallel", "arbitrary"), 32),
        name="sb_prompt",
    )(bias.astype(F32), q3, k3, v3)
    return out.reshape(b * s, width)


def _sb_sample_kernel(pt_ref, bias_ref, q_ref, knew_ref, vnew_ref, *refs, n_pages, n_new):
    del pt_ref
    k_pages = refs[:n_pages]
    v_pages = refs[n_pages:2 * n_pages]
    o_ref = refs[2 * n_pages]
    rows = q_ref.shape[1]
    nh = o_ref.shape[2]
    q = q_ref[0].astype(BF16)
    bias = bias_ref[...]
    minus_later = _minus_later_twice(HEAD)

    def own_head(width):
        r = lax.broadcasted_iota(jnp.int32, (rows, width), 0)
        c = lax.broadcasted_iota(jnp.int32, (rows, width), 1)
        return r, c, (r & (nh - 1)) == (c & (nh - 1))

    r, c, own = own_head(knew_ref.shape[1])
    shift = int(math.log2(nh))
    new_mask = own & ((c >> shift) < (r >> shift)) & ((c >> shift) < n_new)
    page_mask = own_head(k_pages[0].shape[2])[2]
    segs = [(lambda: knew_ref[0], lambda: vnew_ref[0], new_mask)]
    segs += [(lambda p=p: k_pages[p][0, 0], lambda p=p: v_pages[p][0, 0], page_mask)
             for p in reversed(range(n_pages))]
    n = len(segs)
    zs, sps, leaves, sums = [None] * n, [None] * n, [None] * n, [None] * n
    csum = jnp.zeros((rows, 1), F32)
    acc = jnp.zeros((rows, HEAD), F32)
    for step in range(n + 2):
        if step < n:
            zs[step] = _dot_nt(q, segs[step][0]().astype(BF16)) + bias
        i = step - 1
        if 0 <= i < n:
            sps[i] = _softplus_fast(zs[i])
            leaves[i] = jnp.where(segs[i][2], sps[i], 0.0)
            sums[i] = [_dot_exact_rhs(leaves[i][:, blk * HEAD:(blk + 1) * HEAD], minus_later)
                       for blk in range(zs[i].shape[1] // HEAD)]
        i = step - 2
        if 0 <= i < n:
            pieces = [None] * len(sums[i])
            for blk in reversed(range(len(sums[i]))):
                pieces[blk] = sums[i][blk] + csum
                csum = csum - jnp.sum(leaves[i][:, blk * HEAD:(blk + 1) * HEAD], axis=-1, keepdims=True)
            survive = pieces[0] if len(pieces) == 1 else jnp.concatenate(pieces, axis=1)
            w = jnp.where(segs[i][2], jnp.exp(zs[i] - sps[i] + survive), 0.0)
            acc = acc + _dot(w.astype(BF16), segs[i][1]().astype(BF16))
            zs[i] = sps[i] = leaves[i] = sums[i] = None
    o_ref[0] = acc.reshape(rows // nh, nh, HEAD)


def _sb_sample(q, k_new, v_new, bias, cache_k, cache_v, layer, page_table, b, s):
    width = q.shape[1]
    nh = width // HEAD
    n_pages = page_table.shape[1]
    rows = s * nh
    new_rows = HEAD
    assert rows <= new_rows and cache_k.shape[2] == PAGE and nh & (nh - 1) == 0

    key_rows = lambda a: jnp.pad(a.reshape(b, rows, HEAD), ((0, 0), (0, new_rows - rows), (0, 0)))
    flat = lambda c: c.reshape(c.shape[0], c.shape[1], PAGE * nh, HEAD)
    bias_rows = jnp.tile(bias.astype(F32), s).reshape(rows, 1)

    def page_spec(p):
        return pl.BlockSpec((1, 1, PAGE * nh, HEAD), lambda bi, pt: (layer, pt[bi, p], 0, 0))

    grid_spec = pltpu.PrefetchScalarGridSpec(
        num_scalar_prefetch=1,
        grid=(b,),
        in_specs=[pl.BlockSpec((rows, 1), lambda bi, pt: (0, 0)),
                  pl.BlockSpec((1, rows, HEAD), lambda bi, pt: (bi, 0, 0)),
                  pl.BlockSpec((1, new_rows, HEAD), lambda bi, pt: (bi, 0, 0)),
                  pl.BlockSpec((1, new_rows, HEAD), lambda bi, pt: (bi, 0, 0))]
        + [page_spec(p) for p in range(n_pages)] * 2,
        out_specs=pl.BlockSpec((1, s, nh, HEAD), lambda bi, pt: (bi, 0, 0, 0)),
    )
    out = pl.pallas_call(
        functools.partial(_sb_sample_kernel, n_pages=n_pages, n_new=s),
        grid_spec=grid_spec,
        out_shape=jax.ShapeDtypeStruct((b, s, nh, HEAD), F32),
        compiler_params=_cparams(("arbitrary",), 56),
        name="sb_sample",
    )(page_table, bias_rows, q.reshape(b, rows, HEAD), key_rows(k_new), key_rows(v_new),
      *([flat(cache_k)] * n_pages), *([flat(cache_v)] * n_pages))
    return out.reshape(b * s, width)


def _gdn_kernel(raw_ref, ab_ref, gz_ref, convw_ref, conv0_ref, s0_ref, alog_ref, dtb_ref, gn_ref,
                o_ref, s_ref, pad_ref, *, c, n_valid):
    n = pl.program_id(1)
    nh = s_ref.shape[1]
    qkw = nh * HEAD

    @pl.when(n == 0)
    def _():
        pad_ref[0:8, :] = jnp.zeros((8, pad_ref.shape[1]), F32)
        pad_ref[5:8, :] = conv0_ref[0]
        s_ref[...] = s0_ref[...]

    pad_ref[8:8 + c, :] = raw_ref[0]
    cw = convw_ref[...]
    conv = pad_ref[5:5 + c, :] * cw[0:1, :]
    for j in range(1, GDN_CONV):
        conv = conv + pad_ref[5 + j:5 + j + c, :] * cw[j:j + 1, :]
    pad_ref[5:8, :] = pad_ref[5 + c:8 + c, :]
    xs = _silu(conv)

    ab = ab_ref[0]
    live = lax.broadcasted_iota(jnp.int32, ab.shape, 0) < n_valid
    g_all = jnp.where(live, -jnp.exp(alog_ref[...]) * _softplus(ab + dtb_ref[...]), 0.0)
    beta_all = jnp.where(live, jax.nn.sigmoid(ab), 0.0)
    rc = lax.broadcasted_iota(jnp.int32, (c, c), 0)
    cc = lax.broadcasted_iota(jnp.int32, (c, c), 1)
    cum_all = _dot_exact_lhs((cc <= rc).astype(BF16), g_all)
    cum_last = cum_all[c - 1:c, :]
    ecum_all = jnp.exp(cum_all)
    eend_all = jnp.exp(cum_last - cum_all)
    glast_all = jnp.exp(cum_last)
    gn = gn_ref[...]

    hpg = min(nh, max(1, GDN_CHUNK // c))
    rows = hpg * c
    groups = [list(range(g0, g0 + hpg)) for g0 in range(0, nh, hpg)]
    ri = lax.broadcasted_iota(jnp.int32, (rows, rows), 0)
    ci = lax.broadcasted_iota(jnp.int32, (rows, rows), 1)
    shift = int(math.log2(c))
    same = (ri >> shift) == (ci >> shift)
    strict = same & (ri > ci)
    incl = same & (ri >= ci)
    upto = incl.astype(BF16)
    eye = (ri == ci).astype(F32)

    def stack(pieces):
        return pieces[0] if len(pieces) == 1 else jnp.concatenate(pieces, axis=0)

    def unit(x):
        return x * lax.rsqrt(jnp.sum(x * x, axis=-1, keepdims=True) + EPS)

    ops = []
    for hs in groups:
        q = unit(stack([xs[:, h * HEAD:(h + 1) * HEAD] for h in hs])) * (HEAD ** -0.5)
        k = unit(stack([xs[:, qkw + h * HEAD:qkw + (h + 1) * HEAD] for h in hs]))
        v = stack([xs[:, 2 * qkw + h * HEAD:2 * qkw + (h + 1) * HEAD] for h in hs])
        col = lambda a, off: stack([a[:, off + h:off + h + 1] for h in hs])
        ops.append(dict(q=q, k=k, v=v, g=col(g_all, 0), beta=col(beta_all, nh),
                        ecum=col(ecum_all, 0), eend=col(eend_all, 0)))
    diffs = [_dot_exact_lhs(upto, jnp.where(strict, jnp.broadcast_to(o["g"], (rows, rows)), 0.0), terms=2)
             for o in ops]
    kks = [_dot_bf(o["k"], o["k"], _dot_nt) for o in ops]
    qks = [_dot_bf(o["q"], o["k"], _dot_nt) for o in ops]
    decays = [jnp.where(incl, jnp.exp(d), 0.0) for d in diffs]
    npows = [jnp.where(strict, o["beta"] * kk * dec, 0.0) for o, kk, dec in zip(ops, kks, decays)]
    invs = [eye - n for n in npows]
    for _ in range(shift - 1):
        npows = [_dot_hp(n, n) for n in npows]
        invs = [inv + _dot_hp(inv, n) for inv, n in zip(invs, npows)]
    sols = [_dot_bf(inv, jnp.concatenate([o["v"] * o["beta"], o["k"] * (o["beta"] * o["ecum"])], axis=-1))
            for inv, o in zip(invs, ops)]
    qdecs = [o["q"] * o["ecum"] for o in ops]
    from_state = []
    for hs, sol, qdec in zip(groups, sols, qdecs):
        for j, h in enumerate(hs):
            r = slice(j * c, (j + 1) * c)
            from_state.append(_dot_bf(jnp.concatenate([sol[r, HEAD:], qdec[r, :]], axis=0), s_ref[0, h]))
    v_news, outs = [], []
    for gi, (hs, sol, qk, dec) in enumerate(zip(groups, sols, qks, decays)):
        mine = from_state[gi * hpg:(gi + 1) * hpg]
        v_new = sol[:, :HEAD] - stack([m[:c, :] for m in mine])
        v_news.append(v_new)
        outs.append(stack([m[c:, :] for m in mine]) + _dot_bf(qk * dec, v_new))
    for hs, o, v_new in zip(groups, ops, v_news):
        k_end = o["k"] * o["eend"]
        for j, h in enumerate(hs):
            r = slice(j * c, (j + 1) * c)
            s_ref[0, h] = glast_all[:, h:h + 1] * s_ref[0, h] + _dot_bf(k_end[r, :], v_new[r, :], _dot_tn)
    for hs, out in zip(groups, outs):
        normed = _rms(out, gn)
        for j, h in enumerate(hs):
            sl = slice(h * HEAD, (h + 1) * HEAD)
            o_ref[0, :, sl] = (normed[j * c:(j + 1) * c, :] * _silu(gz_ref[0, :, sl])).astype(o_ref.dtype)


def _gdn(raw, ab, gz, conv_w, conv0, s0, a_log, dt_bias, out_gain, b, s, c, n_valid):
    nh = s0.shape[1]
    ch = raw.shape[2]
    pad_lane = lambda a: jnp.pad(a.astype(F32), (0, HEAD - a.shape[0])).reshape(1, HEAD)
    out, state = pl.pallas_call(
        functools.partial(_gdn_kernel, c=c, n_valid=n_valid),
        grid=(b, s // c),
        in_specs=[pl.BlockSpec((1, c, ch), lambda bi, n: (bi, n, 0)),
                  pl.BlockSpec((1, c, HEAD), lambda bi, n: (bi, n, 0)),
                  pl.BlockSpec((1, c, nh * HEAD), lambda bi, n: (bi, n, 0)),
                  pl.BlockSpec((GDN_CONV, ch), lambda bi, n: (0, 0)),
                  pl.BlockSpec((1, GDN_CONV - 1, ch), lambda bi, n: (bi, 0, 0)),
                  pl.BlockSpec((1, nh, HEAD, HEAD), lambda bi, n: (bi, 0, 0, 0)),
                  pl.BlockSpec((1, HEAD), lambda bi, n: (0, 0)),
                  pl.BlockSpec((1, HEAD), lambda bi, n: (0, 0)),
                  pl.BlockSpec((1, HEAD), lambda bi, n: (0, 0))],
        out_specs=[pl.BlockSpec((1, c, nh * HEAD), lambda bi, n: (bi, n, 0)),
                   pl.BlockSpec((1, nh, HEAD, HEAD), lambda bi, n: (bi, 0, 0, 0))],
        out_shape=[jax.ShapeDtypeStruct((b, s, nh * HEAD), BF16),
                   jax.ShapeDtypeStruct(s0.shape, F32)],
        scratch_shapes=[pltpu.VMEM((8 + c, ch), F32)],
        compiler_params=_cparams(("parallel", "arbitrary"), 32),
        name="gdn",
    )(raw, ab, gz, conv_w, conv0, s0, pad_lane(a_log), pad_lane(dt_bias), out_gain.reshape(1, HEAD))
    return out, state


def _out_kernel(*refs, n_a, with_norm):
    a_refs = refs[:n_a]
    w_ref, x_ref, g_ref = refs[n_a:n_a + 3]
    outs = refs[n_a + 3:]
    acc = x_ref[...]
    off = 0
    for a in a_refs:
        ka = a.shape[1]
        acc = acc + _dot(a[...].astype(BF16), w_ref[off:off + ka, :])
        off += ka
    outs[0][...] = acc
    if with_norm:
        outs[1][...] = _rms(acc, g_ref[...]).astype(outs[1].dtype)


def _proj_residual(acts, w_bf16, x, next_gain=None):
    t, d = x.shape
    tm = min(t, 256)
    with_norm = next_gain is not None
    g = (next_gain if with_norm else jnp.ones((d,), F32)).reshape(1, d)
    out_shape = [jax.ShapeDtypeStruct((t, d), F32)]
    out_specs = [pl.BlockSpec((tm, d), lambda i: (i, 0))]
    if with_norm:
        out_shape.append(jax.ShapeDtypeStruct((t, d), BF16))
        out_specs.append(pl.BlockSpec((tm, d), lambda i: (i, 0)))
    res = pl.pallas_call(
        functools.partial(_out_kernel, n_a=len(acts), with_norm=with_norm),
        grid=(t // tm,),
        in_specs=[pl.BlockSpec((tm, a.shape[1]), lambda i: (i, 0)) for a in acts]
        + [pl.BlockSpec(w_bf16.shape, lambda i: (0, 0)),
           pl.BlockSpec((tm, d), lambda i: (i, 0)),
           pl.BlockSpec((1, d), lambda i: (0, 0))],
        out_specs=out_specs,
        out_shape=out_shape,
        compiler_params=_cparams(("parallel",), 48),
        name="proj_residual",
    )(*acts, w_bf16, x, g)
    return res if with_norm else (res[0], None)


def _xattn_prompt_kernel(q_ref, k_ref, v_ref, o_ref, *, scale):
    for h in range(q_ref.shape[2] // HEAD):
        sl = slice(h * HEAD, (h + 1) * HEAD)
        logits = _dot_nt(q_ref[0, :, sl], k_ref[0, :, sl].astype(BF16)) * scale
        p = jnp.exp(logits - jnp.max(logits, axis=-1, keepdims=True))
        attn = p / jnp.sum(p, axis=-1, keepdims=True)
        o_ref[0, :, sl] = _dot(attn.astype(BF16), v_ref[0, :, sl].astype(BF16)).astype(o_ref.dtype)


def _xattn_prompt(xq, mem_k, mem_v, b, s):
    width = xq.shape[1]
    m = mem_k.shape[1]
    tq = min(s, 512)
    out = pl.pallas_call(
        functools.partial(_xattn_prompt_kernel, scale=HEAD ** -0.5),
        grid=(b, s // tq),
        in_specs=[pl.BlockSpec((1, tq, width), lambda bi, i: (bi, i, 0)),
                  pl.BlockSpec((1, m, width), lambda bi, i: (bi, 0, 0)),
                  pl.BlockSpec((1, m, width), lambda bi, i: (bi, 0, 0))],
        out_specs=pl.BlockSpec((1, tq, width), lambda bi, i: (bi, i, 0)),
        out_shape=jax.ShapeDtypeStruct((b, s, width), BF16),
        compiler_params=_cparams(("parallel", "arbitrary"), 32),
        name="xattn_prompt",
    )(xq.reshape(b, s, width), mem_k, mem_v)
    return out.reshape(b * s, width)


def _xattn_sample_kernel(q_ref, k_ref, v_ref, o_ref, *, n_new, scale):
    bb, nh = q_ref.shape[0], q_ref.shape[1]
    m = k_ref.shape[2] // nh
    pairs = [(i, h) for i in range(bb) for h in range(nh)]
    logits = [_dot_nt(q_ref[i, h].astype(BF16), k_ref[0, i, pl.ds(h, m, stride=nh), :].astype(BF16)) * scale
              for i, h in pairs]
    ps = [jnp.exp(lg - jnp.max(lg, axis=-1, keepdims=True)) for lg in logits]
    attns = [(p / jnp.sum(p, axis=-1, keepdims=True)).astype(BF16) for p in ps]
    for (i, h), attn in zip(pairs, attns):
        o_h = _dot(attn, v_ref[0, i, pl.ds(h, m, stride=nh), :].astype(BF16))
        o_ref[i, :, h * HEAD:(h + 1) * HEAD] = o_h[0:n_new, :]


def _xattn_sample(xq, mem_k, mem_v, layer, b, s):
    width = xq.shape[1]
    nh = width // HEAD
    m = mem_k.shape[2]
    qrows = 8
    bb = math.gcd(b, 4)
    q4 = jnp.pad(xq.reshape(b, s, nh, HEAD).transpose(0, 2, 1, 3), ((0, 0), (0, 0), (0, qrows - s), (0, 0)))
    mem_spec = pl.BlockSpec((1, bb, m * nh, HEAD), lambda bi: (layer, bi, 0, 0))
    flat = lambda c: c.reshape(c.shape[0], c.shape[1], m * nh, HEAD)
    out = pl.pallas_call(
        functools.partial(_xattn_sample_kernel, n_new=s, scale=HEAD ** -0.5),
        grid=(b // bb,),
        in_specs=[pl.BlockSpec((bb, nh, qrows, HEAD), lambda bi: (bi, 0, 0, 0)), mem_spec, mem_spec],
        out_specs=pl.BlockSpec((bb, s, width), lambda bi: (bi, 0, 0)),
        out_shape=jax.ShapeDtypeStruct((b, s, width), F32),
        compiler_params=_cparams(("parallel",), 32),
        name="xattn_sample",
    )(q4, flat(mem_k), flat(mem_v))
    return out.reshape(b * s, width)


def _ffn_kernel(x_ref, g_ref, wg_ref, wu_ref, wd_ref, o_ref, h_ref):
    j = pl.program_id(1)

    @pl.when(j == 0)
    def _():
        x = x_ref[...]
        h_ref[...] = _rms(x, g_ref[...]).astype(h_ref.dtype)
        o_ref[...] = x

    h = h_ref[...]
    gate = _dot(h, wg_ref[...].astype(BF16))
    up = _dot(h, wu_ref[...].astype(BF16))
    act = (_silu(gate) * up).astype(BF16)
    o_ref[...] += _dot(act, wd_ref[...].astype(BF16))


def _ffn(x, gain, w_gate_up, w_down):
    t, d = x.shape
    f = w_down.shape[0]
    tm = min(t, 1024)
    tf = 256
    assert f % tf == 0 and t % tm == 0
    nf = f // tf
    return pl.pallas_call(
        _ffn_kernel,
        grid=(t // tm, nf),
        in_specs=[pl.BlockSpec((tm, d), lambda i, j: (i, 0)),
                  pl.BlockSpec((1, d), lambda i, j: (0, 0)),
                  pl.BlockSpec((d, tf), lambda i, j: (0, j)),
                  pl.BlockSpec((d, tf), lambda i, j: (0, nf + j)),
                  pl.BlockSpec((tf, d), lambda i, j: (j, 0))],
        out_specs=pl.BlockSpec((tm, d), lambda i, j: (i, 0)),
        out_shape=jax.ShapeDtypeStruct((t, d), F32),
        scratch_shapes=[pltpu.VMEM((tm, d), BF16)],
        compiler_params=_cparams(("parallel", "arbitrary"), 60),
        name="ffn",
    )(x, gain.reshape(1, d), w_gate_up, w_gate_up, w_down)


def _decoder_layer(x, b, s, p, sb_attend, xattend, conv0, s0):
    d = x.shape[1]
    sbw = SB_HEADS * HEAD
    gw = GDN_HEADS * HEAD
    h = _rmsnorm_bf16(x, p["norm_mix_g"])
    q = _proj(h, p["w_in"], 0, sbw, p["sb_q_dtype"], p["sb_q_norm_g"] * (HEAD ** -0.5))
    k_new = _proj(h, p["w_in"], sbw, sbw, F32, p["sb_k_norm_g"])
    v_new = _proj(h, p["w_in"], 2 * sbw, sbw, F32)
    raw = _proj(h, p["w_in"], 3 * sbw, 3 * gw, F32)
    gz = _proj(h, p["w_gz"], 0, gw, F32)
    ab = _proj(h, p["w_ab"], 0, HEAD, F32)
    o_sb = sb_attend(q, k_new, v_new)

    c = math.gcd(s, GDN_CHUNK)
    raw3, ab3, gz3 = raw.reshape(b, s, 3 * gw), ab.reshape(b, s, HEAD), gz.reshape(b, s, gw)
    conv_new = raw3[:, s - (GDN_CONV - 1):, :]
    n_valid = c
    if c < 8:
        padn = ((0, 0), (0, 8 - s), (0, 0))
        raw3, ab3, gz3 = jnp.pad(raw3, padn), jnp.pad(ab3, padn), jnp.pad(gz3, padn)
        c = 8
    o_gdn, state = _gdn(raw3, ab3, gz3, p["gdn_conv_w"], conv0, s0, p["gdn_a_log"], p["gdn_dt_bias"],
                        p["gdn_out_norm_g"], b, raw3.shape[1], c, n_valid)
    o_gdn = o_gdn[:, :s, :].reshape(b * s, gw)

    x1, hx = _proj_residual([o_sb, o_gdn], p["w_out_bf"], x, p["norm_x_g"])
    xq = _proj(hx, p["w_xq_bf"], 0, X_HEADS * HEAD, p["xq_dtype"], p["x_q_norm_g"])
    xo = xattend(xq)
    x2, _ = _proj_residual([xo], p["w_xo_bf"], x1)
    y = _ffn(x2, p["norm_ffn_g"], p["w_gate_up"], p["w_down"])
    return y, k_new, v_new, state, conv_new


def kernel(x_prompt, x_sample, mem_prompt, cache_sb_k, cache_sb_v, page_table, state_gdn, state_gdn_conv, cache_mem_k, cache_mem_v, norm_mix_g, w_in, sb_q_norm_g, sb_k_norm_g, sb_logit_bias, gdn_conv_w, gdn_a_log, gdn_dt_bias, gdn_out_norm_g, w_out, norm_x_g, norm_mem_g, w_xq, w_mk, w_mv, x_q_norm_g, x_k_norm_g, w_xo, norm_ffn_g, w_gate_up, w_down):
    bp, sp_, d = x_prompt.shape
    bs, ss, _ = x_sample.shape
    depth = w_in.shape[0]
    m = mem_prompt.shape[1]
    sbw = SB_HEADS * HEAD
    gw = GDN_HEADS * HEAD
    xw = X_HEADS * HEAD
    ch = 3 * gw
    ab0 = 3 * sbw + ch
    yp = x_prompt.reshape(bp * sp_, d)
    ys = x_sample.reshape(bs * ss, d)
    outs = [[] for _ in range(10)]
    for l in range(depth):
        p = {
            "norm_mix_g": norm_mix_g[l], "w_in": w_in[l], "sb_q_norm_g": sb_q_norm_g[l], "sb_k_norm_g": sb_k_norm_g[l],
            "gdn_conv_w": gdn_conv_w[l], "gdn_a_log": gdn_a_log[l], "gdn_dt_bias": gdn_dt_bias[l],
            "gdn_out_norm_g": gdn_out_norm_g[l], "norm_x_g": norm_x_g[l], "x_q_norm_g": x_q_norm_g[l],
            "norm_ffn_g": norm_ffn_g[l], "w_gate_up": w_gate_up[l], "w_down": w_down[l],
            "w_gz": w_in[l][:, ab0 + 2 * GDN_HEADS:],
            "w_ab": jnp.pad(w_in[l][:, ab0:ab0 + 2 * GDN_HEADS], ((0, 0), (0, HEAD - 2 * GDN_HEADS))),
            "w_out_bf": w_out[l].astype(BF16), "w_xq_bf": w_xq[l].astype(BF16), "w_xo_bf": w_xo[l].astype(BF16),
        }
        bias = sb_logit_bias[l]
        hm = _rmsnorm_bf16(mem_prompt.reshape(bp * m, d), norm_mem_g[l])
        mem_k = _proj(hm, w_mk[l], 0, xw, F32, x_k_norm_g[l]).reshape(bp, m, xw)
        mem_v = _proj(hm, w_mv[l], 0, xw, F32).reshape(bp, m, xw)

        pp = dict(p, sb_q_dtype=BF16, xq_dtype=BF16)
        yp, kp, vp, stp, cvp = _decoder_layer(
            yp, bp, sp_, pp,
            lambda q, k, v: _sb_prompt(q, k, v, bias, bp, sp_),
            lambda xq: _xattn_prompt(xq, mem_k, mem_v, bp, sp_),
            jnp.zeros((bp, GDN_CONV - 1, ch), F32), jnp.zeros((bp, GDN_HEADS, HEAD, HEAD), F32))

        ps = dict(p, sb_q_dtype=F32, xq_dtype=F32)
        ys, ks, vs, sts, cvs = _decoder_layer(
            ys, bs, ss, ps,
            lambda q, k, v: _sb_sample(q, k, v, bias, cache_sb_k, cache_sb_v, l, page_table, bs, ss),
            lambda xq: _xattn_sample(xq, cache_mem_k, cache_mem_v, l, bs, ss),
            state_gdn_conv[l], state_gdn[l])

        for acc, val in zip(outs, (
                kp.reshape(bp, sp_, SB_HEADS, HEAD), vp.reshape(bp, sp_, SB_HEADS, HEAD),
                ks.reshape(bs, ss, SB_HEADS, HEAD), vs.reshape(bs, ss, SB_HEADS, HEAD),
                stp.astype(state_gdn.dtype), cvp, sts.astype(state_gdn.dtype), cvs,
                mem_k.reshape(bp, m, X_HEADS, HEAD), mem_v.reshape(bp, m, X_HEADS, HEAD))):
            acc.append(val)
    return (yp.reshape(bp, sp_, d), ys.reshape(bs, ss, d)) + tuple(jnp.stack(o) for o in outs)
```

```python
import functools
import math

import jax
import jax.numpy as jnp
from jax import lax
from jax.experimental import pallas as pl
from jax.experimental.pallas import tpu as pltpu

F32 = jnp.float32
BF16 = jnp.bfloat16
EPS = 1e-6
HEAD = 128
SB_HEADS = 8
GDN_HEADS = 8
X_HEADS = 4
PAGE = 128
GDN_CONV = 4
GDN_CHUNK = 64
MIB = 1 << 20


def _cparams(sem, vmem_mib):
    return pltpu.CompilerParams(dimension_semantics=sem, vmem_limit_bytes=vmem_mib * MIB)


def _dot(a, b):
    return jnp.dot(a, b, preferred_element_type=F32)


def _dot_nt(a, b):
    return lax.dot_general(a, b, (((1,), (1,)), ((), ())), preferred_element_type=F32)


def _dot_tn(a, b):
    return lax.dot_general(a, b, (((0,), (0,)), ((), ())), preferred_element_type=F32)


def _split2(a):
    hi = a.astype(BF16)
    lo = (a - hi.astype(F32)).astype(BF16)
    return hi, lo


def _dot_hp(a, b, dot=_dot):
    ah, al = _split2(a)
    bh, bl = _split2(b)
    return dot(ah, bh) + (dot(ah, bl) + dot(al, bh))


def _dot_bf(a, b, dot=_dot):
    return dot(a.astype(BF16), b.astype(BF16))


def _dot_exact_lhs(m01, b, terms=3):
    acc = None
    for _ in range(terms):
        piece = b.astype(BF16)
        b = b - piece.astype(F32)
        part = _dot(m01, piece)
        acc = part if acc is None else acc + part
    return acc


def _dot_exact_rhs(a, m_twice):
    hi, lo = _split2(a)
    return _dot(jnp.concatenate([hi, lo], axis=1), m_twice)


def _softplus(z):
    return jnp.maximum(z, 0.0) + jnp.log1p(jnp.exp(-jnp.abs(z)))


def _softplus_fast(z):
    return jnp.maximum(z, 0.0) + jnp.log(1.0 + jnp.exp(-jnp.abs(z)))


def _silu(x):
    return x * jax.nn.sigmoid(x)


def _rms(y, g):
    ms = jnp.mean(y * y, axis=-1, keepdims=True)
    return y * lax.rsqrt(ms + EPS) * g


def _rmsnorm_kernel(x_ref, g_ref, o_ref):
    o_ref[...] = _rms(x_ref[...], g_ref[...]).astype(o_ref.dtype)


def _rmsnorm_bf16(x, g):
    t, d = x.shape
    tm = min(t, 512)
    return pl.pallas_call(
        _rmsnorm_kernel,
        grid=(t // tm,),
        in_specs=[pl.BlockSpec((tm, d), lambda i: (i, 0)), pl.BlockSpec((1, d), lambda i: (0, 0))],
        out_specs=pl.BlockSpec((tm, d), lambda i: (i, 0)),
        out_shape=jax.ShapeDtypeStruct((t, d), BF16),
        compiler_params=_cparams(("parallel",), 32),
        name="rmsnorm",
    )(x, g.reshape(1, d))


def _proj_kernel(x_ref, w_ref, g_ref, o_ref, *, head_norm):
    acc = _dot(x_ref[...], w_ref[...].astype(BF16))
    if head_norm:
        g = g_ref[...]
        for s in range(acc.shape[1] // HEAD):
            sl = slice(s * HEAD, (s + 1) * HEAD)
            o_ref[:, sl] = _rms(acc[:, sl], g).astype(o_ref.dtype)
    else:
        o_ref[...] = acc.astype(o_ref.dtype)


def _proj(h, w, col0, n_cols, out_dtype, head_gain=None):
    t, k = h.shape
    tm = min(t, 2048)
    tn = min(512, n_cols)
    assert col0 % tn == 0 and n_cols % tn == 0 and t % tm == 0
    cb = col0 // tn
    g = jnp.ones((1, HEAD), F32) if head_gain is None else head_gain.reshape(1, HEAD).astype(F32)
    return pl.pallas_call(
        functools.partial(_proj_kernel, head_norm=head_gain is not None),
        grid=(t // tm, n_cols // tn),
        in_specs=[pl.BlockSpec((tm, k), lambda i, j: (i, 0)),
                  pl.BlockSpec((k, tn), lambda i, j: (0, cb + j)),
                  pl.BlockSpec((1, HEAD), lambda i, j: (0, 0))],
        out_specs=pl.BlockSpec((tm, tn), lambda i, j: (i, j)),
        out_shape=jax.ShapeDtypeStruct((t, n_cols), out_dtype),
        compiler_params=_cparams(("parallel", "arbitrary"), 48),
        name="proj",
    )(h, w, g)


IN_TN = 256


def _in_proj_kernel(h_ref, w_ref, qg_ref, kg_ref, q_ref, k_ref, v_ref, raw_ref, gz_ref, ab_ref, *, edges):
    j = pl.program_id(1)
    acc = _dot_nt(h_ref[...], w_ref[...].astype(BF16))

    def head_normed(g_ref):
        g = g_ref[...]
        return jnp.concatenate(
            [_rms(acc[:, s * HEAD:(s + 1) * HEAD], g) for s in range(acc.shape[1] // HEAD)], axis=1)

    def when_in(lo, hi, store):
        pl.when((j >= lo) & (j < hi))(store)

    def put_q():
        q_ref[...] = head_normed(qg_ref).astype(q_ref.dtype)

    def put_k():
        k_ref[...] = head_normed(kg_ref)

    def put(ref):
        def store():
            ref[...] = acc
        return store

    stores = (put_q, put_k, put(v_ref), put(raw_ref), put(gz_ref), put(ab_ref))
    for lo, hi, store in zip((0,) + edges[:-1], edges, stores):
        when_in(lo, hi, store)


def _in_proj(h, w_rows, q_gain, k_gain, q_dtype, widths):
    t, k = h.shape
    tn = IN_TN
    tm = min(t, 2048)
    assert all(w % tn == 0 for w in widths) and t % tm == 0
    nblk = [w // tn for w in widths]
    edges = tuple(int(sum(nblk[:i + 1])) for i in range(len(nblk)))
    starts = (0,) + edges[:-1]

    def out_spec(lo, n):
        return pl.BlockSpec((tm, tn), lambda i, j: (i, jnp.clip(j - lo, 0, n - 1)))

    dtypes = (q_dtype, F32, F32, F32, F32, F32)
    gain = lambda g: g.reshape(1, HEAD).astype(F32)
    return pl.pallas_call(
        functools.partial(_in_proj_kernel, edges=edges),
        grid=(t // tm, edges[-1]),
        in_specs=[pl.BlockSpec((tm, k), lambda i, j: (i, 0)),
                  pl.BlockSpec((tn, k), lambda i, j: (j, 0)),
                  pl.BlockSpec((1, HEAD), lambda i, j: (0, 0)),
                  pl.BlockSpec((1, HEAD), lambda i, j: (0, 0))],
        out_specs=[out_spec(lo, n) for lo, n in zip(starts, nblk)],
        out_shape=[jax.ShapeDtypeStruct((t, w), dt) for w, dt in zip(widths, dtypes)],
        compiler_params=_cparams(("parallel", "arbitrary"), 56),
        name="in_proj",
    )(h, w_rows, gain(q_gain), gain(k_gain))


def _minus_later_twice(n):
    row = lax.broadcasted_iota(jnp.int32, (n, n), 0)
    col = lax.broadcasted_iota(jnp.int32, (n, n), 1)
    m = jnp.where(row > col, -1.0, 0.0).astype(BF16)
    return jnp.concatenate([m, m], axis=0)


def _sb_prompt_kernel(bias_ref, q_ref, k_ref, v_ref, o_ref, *, tq, hpb):
    hb = pl.program_id(1)
    i = pl.program_id(2)
    row = lax.broadcasted_iota(jnp.int32, (tq, tq), 0)
    col = lax.broadcasted_iota(jnp.int32, (tq, tq), 1)
    minus_later = _minus_later_twice(tq)
    causal = col < row
    heads = range(hpb)
    qs = [q_ref[0, :, g * HEAD:(g + 1) * HEAD] for g in heads]
    biases = [bias_ref[hb * hpb + g] for g in heads]

    def block(j, carry, masked):
        start = pl.multiple_of(j * tq, tq)
        kb = k_ref[0, pl.ds(start, tq), :].astype(BF16)
        vb = v_ref[0, pl.ds(start, tq), :].astype(BF16)
        zs = [_dot_nt(qs[g], kb[:, g * HEAD:(g + 1) * HEAD]) + biases[g] for g in heads]
        sps = [_softplus_fast(z) for z in zs]
        leaves = [jnp.where(causal, sp, 0.0) if masked else sp for sp in sps]
        survs = [_dot_exact_rhs(lv, minus_later) for lv in leaves]
        out = []
        for g in heads:
            csum, acc = carry[g]
            w = jnp.exp(zs[g] - sps[g] + survs[g] + csum)
            if masked:
                w = jnp.where(causal, w, 0.0)
            acc = acc + _dot(w.astype(BF16), vb[:, g * HEAD:(g + 1) * HEAD])
            out.append((csum - jnp.sum(leaves[g], axis=-1, keepdims=True), acc))
        return tuple(out)

    init = tuple((jnp.zeros((tq, 1), F32), jnp.zeros((tq, HEAD), F32)) for _ in heads)
    carry = block(i, init, True)
    carry = lax.fori_loop(0, i, lambda jj, c: block(i - 1 - jj, c, False), carry)
    for g in heads:
        o_ref[0, :, g * HEAD:(g + 1) * HEAD] = carry[g][1].astype(o_ref.dtype)


def _sb_prompt(q, k, v, bias, b, s):
    width = q.shape[1]
    nh = width // HEAD
    hpb = 4
    tq = min(s, 256)
    assert s % tq == 0 and nh % hpb == 0
    bw = hpb * HEAD
    q3, k3, v3 = (a.reshape(b, s, width) for a in (q, k, v))
    out = pl.pallas_call(
        functools.partial(_sb_prompt_kernel, tq=tq, hpb=hpb),
        grid=(b, nh // hpb, s // tq),
        in_specs=[pl.BlockSpec(memory_space=pltpu.SMEM),
                  pl.BlockSpec((1, tq, bw), lambda bi, h, i: (bi, i, h)),
                  pl.BlockSpec((1, s, bw), lambda bi, h, i: (bi, 0, h)),
                  pl.BlockSpec((1, s, bw), lambda bi, h, i: (bi, 0, h))],
        out_specs=pl.BlockSpec((1, tq, bw), lambda bi, h, i: (bi, i, h)),
        out_shape=jax.ShapeDtypeStruct((b, s, width), BF16),
        compiler_params=_cparams(("parallel", "parallel", "arbitrary"), 32),
        name="sb_prompt",
    )(bias.astype(F32), q3, k3, v3)
    return out.reshape(b * s, width)


def _sb_sample_kernel(pt_ref, bias_ref, q_ref, knew_ref, vnew_ref, *refs, n_pages, n_new):
    del pt_ref
    k_pages = refs[:n_pages]
    v_pages = refs[n_pages:2 * n_pages]
    o_ref = refs[2 * n_pages]
    rows = q_ref.shape[1]
    nh = o_ref.shape[2]
    q = q_ref[0].astype(BF16)
    bias = bias_ref[...]
    minus_later = _minus_later_twice(HEAD)

    def own_head(width):
        r = lax.broadcasted_iota(jnp.int32, (rows, width), 0)
        c = lax.broadcasted_iota(jnp.int32, (rows, width), 1)
        return r, c, (r & (nh - 1)) == (c & (nh - 1))

    r, c, own = own_head(knew_ref.shape[1])
    shift = int(math.log2(nh))
    new_mask = own & ((c >> shift) < (r >> shift)) & ((c >> shift) < n_new)
    page_mask = own_head(k_pages[0].shape[2])[2]
    segs = [(lambda: knew_ref[0], lambda: vnew_ref[0], new_mask)]
    segs += [(lambda p=p: k_pages[p][0, 0], lambda p=p: v_pages[p][0, 0], page_mask)
             for p in reversed(range(n_pages))]
    n = len(segs)
    zs, sps, leaves, sums = [None] * n, [None] * n, [None] * n, [None] * n
    csum = jnp.zeros((rows, 1), F32)
    acc = jnp.zeros((rows, HEAD), F32)
    for step in range(n + 2):
        if step < n:
            zs[step] = _dot_nt(q, segs[step][0]().astype(BF16)) + bias
        i = step - 1
        if 0 <= i < n:
            sps[i] = _softplus_fast(zs[i])
            leaves[i] = jnp.where(segs[i][2], sps[i], 0.0)
            sums[i] = [_dot_exact_rhs(leaves[i][:, blk * HEAD:(blk + 1) * HEAD], minus_later)
                       for blk in range(zs[i].shape[1] // HEAD)]
        i = step - 2
        if 0 <= i < n:
            pieces = [None] * len(sums[i])
            for blk in reversed(range(len(sums[i]))):
                pieces[blk] = sums[i][blk] + csum
                csum = csum - jnp.sum(leaves[i][:, blk * HEAD:(blk + 1) * HEAD], axis=-1, keepdims=True)
            survive = pieces[0] if len(pieces) == 1 else jnp.concatenate(pieces, axis=1)
            w = jnp.where(segs[i][2], jnp.exp(zs[i] - sps[i] + survive), 0.0)
            acc = acc + _dot(w.astype(BF16), segs[i][1]().astype(BF16))
            zs[i] = sps[i] = leaves[i] = sums[i] = None
    o_ref[0] = acc.reshape(rows // nh, nh, HEAD)


def _sb_sample(q, k_new, v_new, bias, cache_k, cache_v, layer, page_table, b, s):
    width = q.shape[1]
    nh = width // HEAD
    n_pages = page_table.shape[1]
    rows = s * nh
    new_rows = HEAD
    assert rows <= new_rows and cache_k.shape[2] == PAGE and nh & (nh - 1) == 0

    key_rows = lambda a: jnp.pad(a.reshape(b, rows, HEAD), ((0, 0), (0, new_rows - rows), (0, 0)))
    flat = lambda c: c.reshape(c.shape[0], c.shape[1], PAGE * nh, HEAD)
    bias_rows = jnp.tile(bias.astype(F32), s).reshape(rows, 1)

    def page_spec(p):
        return pl.BlockSpec((1, 1, PAGE * nh, HEAD), lambda bi, pt: (layer, pt[bi, p], 0, 0))

    grid_spec = pltpu.PrefetchScalarGridSpec(
        num_scalar_prefetch=1,
        grid=(b,),
        in_specs=[pl.BlockSpec((rows, 1), lambda bi, pt: (0, 0)),
                  pl.BlockSpec((1, rows, HEAD), lambda bi, pt: (bi, 0, 0)),
                  pl.BlockSpec((1, new_rows, HEAD), lambda bi, pt: (bi, 0, 0)),
                  pl.BlockSpec((1, new_rows, HEAD), lambda bi, pt: (bi, 0, 0))]
        + [page_spec(p) for p in range(n_pages)] * 2,
        out_specs=pl.BlockSpec((1, s, nh, HEAD), lambda bi, pt: (bi, 0, 0, 0)),
    )
    out = pl.pallas_call(
        functools.partial(_sb_sample_kernel, n_pages=n_pages, n_new=s),
        grid_spec=grid_spec,
        out_shape=jax.ShapeDtypeStruct((b, s, nh, HEAD), F32),
        compiler_params=_cparams(("arbitrary",), 56),
        name="sb_sample",
    )(page_table, bias_rows, q.reshape(b, rows, HEAD), key_rows(k_new), key_rows(v_new),
      *([flat(cache_k)] * n_pages), *([flat(cache_v)] * n_pages))
    return out.reshape(b * s, width)


def _gdn_kernel(raw_ref, ab_ref, gz_ref, convw_ref, conv0_ref, s0_ref, alog_ref, dtb_ref, gn_ref,
                o_ref, s_ref, pad_ref, *, c, n_valid):
    n = pl.program_id(1)
    nh = s_ref.shape[1]
    qkw = nh * HEAD

    @pl.when(n == 0)
    def _():
        pad_ref[0:8, :] = jnp.zeros((8, pad_ref.shape[1]), F32)
        pad_ref[5:8, :] = conv0_ref[0]
        s_ref[...] = s0_ref[...]

    pad_ref[8:8 + c, :] = raw_ref[0]
    cw = convw_ref[...]
    conv = pad_ref[5:5 + c, :] * cw[0:1, :]
    for j in range(1, GDN_CONV):
        conv = conv + pad_ref[5 + j:5 + j + c, :] * cw[j:j + 1, :]
    pad_ref[5:8, :] = pad_ref[5 + c:8 + c, :]
    xs = _silu(conv)

    ab = ab_ref[0]
    live = lax.broadcasted_iota(jnp.int32, ab.shape, 0) < n_valid
    g_all = jnp.where(live, -jnp.exp(alog_ref[...]) * _softplus(ab + dtb_ref[...]), 0.0)
    beta_all = jnp.where(live, jax.nn.sigmoid(ab), 0.0)
    rc = lax.broadcasted_iota(jnp.int32, (c, c), 0)
    cc = lax.broadcasted_iota(jnp.int32, (c, c), 1)
    cum_all = _dot_exact_lhs((cc <= rc).astype(BF16), g_all)
    cum_last = cum_all[c - 1:c, :]
    ecum_all = jnp.exp(cum_all)
    eend_all = jnp.exp(cum_last - cum_all)
    glast_all = jnp.exp(cum_last)
    gn = gn_ref[...]

    hpg = min(nh, max(1, GDN_CHUNK // c))
    rows = hpg * c
    groups = [list(range(g0, g0 + hpg)) for g0 in range(0, nh, hpg)]
    ri = lax.broadcasted_iota(jnp.int32, (rows, rows), 0)
    ci = lax.broadcasted_iota(jnp.int32, (rows, rows), 1)
    shift = int(math.log2(c))
    same = (ri >> shift) == (ci >> shift)
    strict = same & (ri > ci)
    incl = same & (ri >= ci)
    upto = incl.astype(BF16)
    eye = (ri == ci).astype(F32)

    def stack(pieces):
        return pieces[0] if len(pieces) == 1 else jnp.concatenate(pieces, axis=0)

    def unit(x):
        return x * lax.rsqrt(jnp.sum(x * x, axis=-1, keepdims=True) + EPS)

    ops = []
    for hs in groups:
        q = unit(stack([xs[:, h * HEAD:(h + 1) * HEAD] for h in hs])) * (HEAD ** -0.5)
        k = unit(stack([xs[:, qkw + h * HEAD:qkw + (h + 1) * HEAD] for h in hs]))
        v = stack([xs[:, 2 * qkw + h * HEAD:2 * qkw + (h + 1) * HEAD] for h in hs])
        col = lambda a, off: stack([a[:, off + h:off + h + 1] for h in hs])
        ops.append(dict(q=q, k=k, v=v, g=col(g_all, 0), beta=col(beta_all, nh),
                        ecum=col(ecum_all, 0), eend=col(eend_all, 0)))
    diffs = [_dot_exact_lhs(upto, jnp.where(strict, jnp.broadcast_to(o["g"], (rows, rows)), 0.0), terms=2)
             for o in ops]
    kks = [_dot_bf(o["k"], o["k"], _dot_nt) for o in ops]
    qks = [_dot_bf(o["q"], o["k"], _dot_nt) for o in ops]
    decays = [jnp.where(incl, jnp.exp(d), 0.0) for d in diffs]
    npows = [jnp.where(strict, o["beta"] * kk * dec, 0.0) for o, kk, dec in zip(ops, kks, decays)]
    invs = [eye - n for n in npows]
    for _ in range(shift - 1):
        npows = [_dot_hp(n, n) for n in npows]
        invs = [inv + _dot_hp(inv, n) for inv, n in zip(invs, npows)]
    sols = [_dot_bf(inv, jnp.concatenate([o["v"] * o["beta"], o["k"] * (o["beta"] * o["ecum"])], axis=-1))
            for inv, o in zip(invs, ops)]
    qdecs = [o["q"] * o["ecum"] for o in ops]
    from_state = []
    for hs, sol, qdec in zip(groups, sols, qdecs):
        for j, h in enumerate(hs):
            r = slice(j * c, (j + 1) * c)
            from_state.append(_dot_bf(jnp.concatenate([sol[r, HEAD:], qdec[r, :]], axis=0), s_ref[0, h]))
    v_news, outs = [], []
    for gi, (hs, sol, qk, dec) in enumerate(zip(groups, sols, qks, decays)):
        mine = from_state[gi * hpg:(gi + 1) * hpg]
        v_new = sol[:, :HEAD] - stack([m[:c, :] for m in mine])
        v_news.append(v_new)
        outs.append(stack([m[c:, :] for m in mine]) + _dot_bf(qk * dec, v_new))
    for hs, o, v_new in zip(groups, ops, v_news):
        k_end = o["k"] * o["eend"]
        for j, h in enumerate(hs):
            r = slice(j * c, (j + 1) * c)
            s_ref[0, h] = glast_all[:, h:h + 1] * s_ref[0, h] + _dot_bf(k_end[r, :], v_new[r, :], _dot_tn)
    for hs, out in zip(groups, outs):
        normed = _rms(out, gn)
        for j, h in enumerate(hs):
            sl = slice(h * HEAD, (h + 1) * HEAD)
            o_ref[0, :, sl] = (normed[j * c:(j + 1) * c, :] * _silu(gz_ref[0, :, sl])).astype(o_ref.dtype)


def _gdn(raw, ab, gz, conv_w, conv0, s0, a_log, dt_bias, out_gain, b, s, c, n_valid):
    nh = s0.shape[1]
    ch = raw.shape[2]
    pad_lane = lambda a: jnp.pad(a.astype(F32), (0, HEAD - a.shape[0])).reshape(1, HEAD)
    out, state = pl.pallas_call(
        functools.partial(_gdn_kernel, c=c, n_valid=n_valid),
        grid=(b, s // c),
        in_specs=[pl.BlockSpec((1, c, ch), lambda bi, n: (bi, n, 0)),
                  pl.BlockSpec((1, c, HEAD), lambda bi, n: (bi, n, 0)),
                  pl.BlockSpec((1, c, nh * HEAD), lambda bi, n: (bi, n, 0)),
                  pl.BlockSpec((GDN_CONV, ch), lambda bi, n: (0, 0)),
                  pl.BlockSpec((1, GDN_CONV - 1, ch), lambda bi, n: (bi, 0, 0)),
                  pl.BlockSpec((1, nh, HEAD, HEAD), lambda bi, n: (bi, 0, 0, 0)),
                  pl.BlockSpec((1, HEAD), lambda bi, n: (0, 0)),
                  pl.BlockSpec((1, HEAD), lambda bi, n: (0, 0)),
                  pl.BlockSpec((1, HEAD), lambda bi, n: (0, 0))],
        out_specs=[pl.BlockSpec((1, c, nh * HEAD), lambda bi, n: (bi, n, 0)),
                   pl.BlockSpec((1, nh, HEAD, HEAD), lambda bi, n: (bi, 0, 0, 0))],
        out_shape=[jax.ShapeDtypeStruct((b, s, nh * HEAD), BF16),
                   jax.ShapeDtypeStruct(s0.shape, F32)],
        scratch_shapes=[pltpu.VMEM((8 + c, ch), F32)],
        compiler_params=_cparams(("parallel", "arbitrary"), 32),
        name="gdn",
    )(raw, ab, gz, conv_w, conv0, s0, pad_lane(a_log), pad_lane(dt_bias), out_gain.reshape(1, HEAD))
    return out, state


def _mix_out_kernel(*refs, n_a):
    a_refs = refs[:n_a]
    w_ref, x_ref, g_ref, wq_ref, qg_ref, x_out, q_out = refs[n_a:]
    acc = x_ref[...]
    off = 0
    for a in a_refs:
        ka = a.shape[1]
        acc = acc + _dot(a[...].astype(BF16), w_ref[off:off + ka, :])
        off += ka
    x_out[...] = acc
    xq = _dot(_rms(acc, g_ref[...]).astype(BF16), wq_ref[...])
    qg = qg_ref[...]
    for s in range(xq.shape[1] // HEAD):
        sl = slice(s * HEAD, (s + 1) * HEAD)
        q_out[:, sl] = _rms(xq[:, sl], qg).astype(q_out.dtype)


def _mix_out(acts, w_bf16, x, next_gain, wq_bf16, q_head_gain):
    t, d = x.shape
    qw = wq_bf16.shape[1]
    tm = min(t, 256)
    return pl.pallas_call(
        functools.partial(_mix_out_kernel, n_a=len(acts)),
        grid=(t // tm,),
        in_specs=[pl.BlockSpec((tm, a.shape[1]), lambda i: (i, 0)) for a in acts]
        + [pl.BlockSpec(w_bf16.shape, lambda i: (0, 0)),
           pl.BlockSpec((tm, d), lambda i: (i, 0)),
           pl.BlockSpec((1, d), lambda i: (0, 0)),
           pl.BlockSpec(wq_bf16.shape, lambda i: (0, 0)),
           pl.BlockSpec((1, HEAD), lambda i: (0, 0))],
        out_specs=[pl.BlockSpec((tm, d), lambda i: (i, 0)), pl.BlockSpec((tm, qw), lambda i: (i, 0))],
        out_shape=[jax.ShapeDtypeStruct((t, d), F32), jax.ShapeDtypeStruct((t, qw), BF16)],
        compiler_params=_cparams(("parallel",), 48),
        name="mix_out",
    )(*acts, w_bf16, x, next_gain.reshape(1, d), wq_bf16, q_head_gain.reshape(1, HEAD))


def _xattn_prompt_kernel(q_ref, k_ref, v_ref, o_ref, *, scale):
    for h in range(q_ref.shape[2] // HEAD):
        sl = slice(h * HEAD, (h + 1) * HEAD)
        logits = _dot_nt(q_ref[0, :, sl], k_ref[0, :, sl].astype(BF16)) * scale
        p = jnp.exp(logits - jnp.max(logits, axis=-1, keepdims=True))
        attn = p / jnp.sum(p, axis=-1, keepdims=True)
        o_ref[0, :, sl] = _dot(attn.astype(BF16), v_ref[0, :, sl].astype(BF16)).astype(o_ref.dtype)


def _xattn_prompt(xq, mem_k, mem_v, b, s):
    width = xq.shape[1]
    m = mem_k.shape[1]
    tq = min(s, 512)
    out = pl.pallas_call(
        functools.partial(_xattn_prompt_kernel, scale=HEAD ** -0.5),
        grid=(b, s // tq),
        in_specs=[pl.BlockSpec((1, tq, width), lambda bi, i: (bi, i, 0)),
                  pl.BlockSpec((1, m, width), lambda bi, i: (bi, 0, 0)),
                  pl.BlockSpec((1, m, width), lambda bi, i: (bi, 0, 0))],
        out_specs=pl.BlockSpec((1, tq, width), lambda bi, i: (bi, i, 0)),
        out_shape=jax.ShapeDtypeStruct((b, s, width), BF16),
        compiler_params=_cparams(("parallel", "arbitrary"), 32),
        name="xattn_prompt",
    )(xq.reshape(b, s, width), mem_k, mem_v)
    return out.reshape(b * s, width)


def _xattn_sample_kernel(q_ref, k_ref, v_ref, o_ref, *, n_new, scale):
    bb, nh = q_ref.shape[0], q_ref.shape[1]
    m = k_ref.shape[2] // nh
    pairs = [(i, h) for i in range(bb) for h in range(nh)]
    logits = [_dot_nt(q_ref[i, h].astype(BF16), k_ref[0, i, pl.ds(h, m, stride=nh), :].astype(BF16)) * scale
              for i, h in pairs]
    ps = [jnp.exp(lg - jnp.max(lg, axis=-1, keepdims=True)) for lg in logits]
    attns = [(p / jnp.sum(p, axis=-1, keepdims=True)).astype(BF16) for p in ps]
    for (i, h), attn in zip(pairs, attns):
        o_h = _dot(attn, v_ref[0, i, pl.ds(h, m, stride=nh), :].astype(BF16))
        o_ref[i, :, h * HEAD:(h + 1) * HEAD] = o_h[0:n_new, :]


def _xattn_sample(xq, mem_k, mem_v, layer, b, s):
    width = xq.shape[1]
    nh = width // HEAD
    m = mem_k.shape[2]
    qrows = 8
    bb = math.gcd(b, 4)
    q4 = jnp.pad(xq.reshape(b, s, nh, HEAD).transpose(0, 2, 1, 3), ((0, 0), (0, 0), (0, qrows - s), (0, 0)))
    mem_spec = pl.BlockSpec((1, bb, m * nh, HEAD), lambda bi: (layer, bi, 0, 0))
    flat = lambda c: c.reshape(c.shape[0], c.shape[1], m * nh, HEAD)
    out = pl.pallas_call(
        functools.partial(_xattn_sample_kernel, n_new=s, scale=HEAD ** -0.5),
        grid=(b // bb,),
        in_specs=[pl.BlockSpec((bb, nh, qrows, HEAD), lambda bi: (bi, 0, 0, 0)), mem_spec, mem_spec],
        out_specs=pl.BlockSpec((bb, s, width), lambda bi: (bi, 0, 0)),
        out_shape=jax.ShapeDtypeStruct((b, s, width), F32),
        compiler_params=_cparams(("parallel",), 32),
        name="xattn_sample",
    )(q4, flat(mem_k), flat(mem_v))
    return out.reshape(b * s, width)


def _ffn_kernel(x_ref, a_ref, wa_ref, g_ref, wg_ref, wu_ref, wd_ref, o_ref, h_ref):
    j = pl.program_id(1)

    @pl.when(j == 0)
    def _():
        x = x_ref[...] + _dot(a_ref[...].astype(BF16), wa_ref[...])
        h_ref[...] = _rms(x, g_ref[...]).astype(h_ref.dtype)
        o_ref[...] = x

    h = h_ref[...]
    gate = _dot(h, wg_ref[...].astype(BF16))
    up = _dot(h, wu_ref[...].astype(BF16))
    act = (_silu(gate) * up).astype(BF16)
    o_ref[...] += _dot(act, wd_ref[...].astype(BF16))


def _ffn(x, attn, w_attn_bf16, gain, w_gate_up, w_down):
    t, d = x.shape
    f = w_down.shape[0]
    tm = min(t, 1024)
    tf = 256
    assert f % tf == 0 and t % tm == 0
    nf = f // tf
    return pl.pallas_call(
        _ffn_kernel,
        grid=(t // tm, nf),
        in_specs=[pl.BlockSpec((tm, d), lambda i, j: (i, 0)),
                  pl.BlockSpec((tm, attn.shape[1]), lambda i, j: (i, 0)),
                  pl.BlockSpec(w_attn_bf16.shape, lambda i, j: (0, 0)),
                  pl.BlockSpec((1, d), lambda i, j: (0, 0)),
                  pl.BlockSpec((d, tf), lambda i, j: (0, j)),
                  pl.BlockSpec((d, tf), lambda i, j: (0, nf + j)),
                  pl.BlockSpec((tf, d), lambda i, j: (j, 0))],
        out_specs=pl.BlockSpec((tm, d), lambda i, j: (i, 0)),
        out_shape=jax.ShapeDtypeStruct((t, d), F32),
        scratch_shapes=[pltpu.VMEM((tm, d), BF16)],
        compiler_params=_cparams(("parallel", "arbitrary"), 60),
        name="ffn",
    )(x, attn, w_attn_bf16, gain.reshape(1, d), w_gate_up, w_gate_up, w_down)


def _decoder_layer(x, b, s, p, sb_attend, xattend, conv0, s0):
    d = x.shape[1]
    sbw = SB_HEADS * HEAD
    gw = GDN_HEADS * HEAD
    h = _rmsnorm_bf16(x, p["norm_mix_g"])
    q, k_new, v_new, raw, gz, ab = _in_proj(
        h, p["w_in_rows"], p["sb_q_norm_g"] * (HEAD ** -0.5), p["sb_k_norm_g"], p["sb_q_dtype"],
        (sbw, sbw, sbw, 3 * gw, gw, IN_TN))
    o_sb = sb_attend(q, k_new, v_new)

    c = math.gcd(s, GDN_CHUNK)
    raw3, ab3, gz3 = raw.reshape(b, s, 3 * gw), ab.reshape(b, s, IN_TN), gz.reshape(b, s, gw)
    conv_new = raw3[:, s - (GDN_CONV - 1):, :]
    n_valid = c
    if c < 8:
        padn = ((0, 0), (0, 8 - s), (0, 0))
        raw3, ab3, gz3 = jnp.pad(raw3, padn), jnp.pad(ab3, padn), jnp.pad(gz3, padn)
        c = 8
    o_gdn, state = _gdn(raw3, ab3, gz3, p["gdn_conv_w"], conv0, s0, p["gdn_a_log"], p["gdn_dt_bias"],
                        p["gdn_out_norm_g"], b, raw3.shape[1], c, n_valid)
    o_gdn = o_gdn[:, :s, :].reshape(b * s, gw)

    x1, xq = _mix_out([o_sb, o_gdn], p["w_out_bf"], x, p["norm_x_g"], p["w_xq_bf"], p["x_q_norm_g"])
    xo = xattend(xq)
    y = _ffn(x1, xo, p["w_xo_bf"], p["norm_ffn_g"], p["w_gate_up"], p["w_down"])
    return y, k_new, v_new, state, conv_new


def kernel(x_prompt, x_sample, mem_prompt, cache_sb_k, cache_sb_v, page_table, state_gdn, state_gdn_conv, cache_mem_k, cache_mem_v, norm_mix_g, w_in, sb_q_norm_g, sb_k_norm_g, sb_logit_bias, gdn_conv_w, gdn_a_log, gdn_dt_bias, gdn_out_norm_g, w_out, norm_x_g, norm_mem_g, w_xq, w_mk, w_mv, x_q_norm_g, x_k_norm_g, w_xo, norm_ffn_g, w_gate_up, w_down):
    bp, sp_, d = x_prompt.shape
    bs, ss, _ = x_sample.shape
    depth = w_in.shape[0]
    m = mem_prompt.shape[1]
    sbw = SB_HEADS * HEAD
    gw = GDN_HEADS * HEAD
    xw = X_HEADS * HEAD
    ch = 3 * gw
    ab0 = 3 * sbw + ch
    yp = x_prompt.reshape(bp * sp_, d)
    ys = x_sample.reshape(bs * ss, d)
    outs = [[] for _ in range(10)]
    for l in range(depth):
        wt = jnp.swapaxes(w_in[l], 0, 1)
        n_logit = 2 * GDN_HEADS
        w_in_rows = jnp.concatenate(
            [wt[:ab0], wt[ab0 + n_logit:], wt[ab0:ab0 + n_logit], jnp.zeros((IN_TN - n_logit, d), wt.dtype)], axis=0)
        p = {
            "norm_mix_g": norm_mix_g[l], "w_in_rows": w_in_rows, "sb_q_norm_g": sb_q_norm_g[l],
            "sb_k_norm_g": sb_k_norm_g[l],
            "gdn_conv_w": gdn_conv_w[l], "gdn_a_log": gdn_a_log[l], "gdn_dt_bias": gdn_dt_bias[l],
            "gdn_out_norm_g": gdn_out_norm_g[l], "norm_x_g": norm_x_g[l], "x_q_norm_g": x_q_norm_g[l],
            "norm_ffn_g": norm_ffn_g[l], "w_gate_up": w_gate_up[l], "w_down": w_down[l],
            "w_out_bf": w_out[l].astype(BF16), "w_xq_bf": w_xq[l].astype(BF16), "w_xo_bf": w_xo[l].astype(BF16),
        }
        bias = sb_logit_bias[l]
        hm = _rmsnorm_bf16(mem_prompt.reshape(bp * m, d), norm_mem_g[l])
        mem_k = _proj(hm, w_mk[l], 0, xw, F32, x_k_norm_g[l]).reshape(bp, m, xw)
        mem_v = _proj(hm, w_mv[l], 0, xw, F32).reshape(bp, m, xw)

        pp = dict(p, sb_q_dtype=BF16)
        yp, kp, vp, stp, cvp = _decoder_layer(
            yp, bp, sp_, pp,
            lambda q, k, v: _sb_prompt(q, k, v, bias, bp, sp_),
            lambda xq: _xattn_prompt(xq, mem_k, mem_v, bp, sp_),
            jnp.zeros((bp, GDN_CONV - 1, ch), F32), jnp.zeros((bp, GDN_HEADS, HEAD, HEAD), F32))

        ps = dict(p, sb_q_dtype=F32)
        ys, ks, vs, sts, cvs = _decoder_layer(
            ys, bs, ss, ps,
            lambda q, k, v: _sb_sample(q, k, v, bias, cache_sb_k, cache_sb_v, l, page_table, bs, ss),
            lambda xq: _xattn_sample(xq, cache_mem_k, cache_mem_v, l, bs, ss),
            state_gdn_conv[l], state_gdn[l])

        for acc, val in zip(outs, (
                kp.reshape(bp, sp_, SB_HEADS, HEAD), vp.reshape(bp, sp_, SB_HEADS, HEAD),
                ks.reshape(bs, ss, SB_HEADS, HEAD), vs.reshape(bs, ss, SB_HEADS, HEAD),
                stp.astype(state_gdn.dtype), cvp, sts.astype(state_gdn.dtype), cvs,
                mem_k.reshape(bp, m, X_HEADS, HEAD), mem_v.reshape(bp, m, X_HEADS, HEAD))):
            acc.append(val)
    return (yp.reshape(bp, sp_, d), ys.reshape(bs, ss, d)) + tuple(jnp.stack(o) for o in outs)
```

```python
import functools
import math

import jax
import jax.numpy as jnp
from jax import lax
from jax.experimental import pallas as pl
from jax.experimental.pallas import tpu as pltpu

F32 = jnp.float32
BF16 = jnp.bfloat16
EPS = 1e-6
HEAD = 128
SB_HEADS = 8
GDN_HEADS = 8
X_HEADS = 4
PAGE = 128
GDN_CONV = 4
GDN_CHUNK = 64
MIB = 1 << 20


def _cparams(sem, vmem_mib):
    return pltpu.CompilerParams(dimension_semantics=sem, vmem_limit_bytes=vmem_mib * MIB)


def _dot(a, b):
    return jnp.dot(a, b, preferred_element_type=F32)


def _dot_nt(a, b):
    return lax.dot_general(a, b, (((1,), (1,)), ((), ())), preferred_element_type=F32)


def _dot_tn(a, b):
    return lax.dot_general(a, b, (((0,), (0,)), ((), ())), preferred_element_type=F32)


def _split2(a):
    hi = a.astype(BF16)
    lo = (a - hi.astype(F32)).astype(BF16)
    return hi, lo


def _dot_hp(a, b, dot=_dot):
    ah, al = _split2(a)
    bh, bl = _split2(b)
    return dot(ah, bh) + (dot(ah, bl) + dot(al, bh))


def _dot_bf(a, b, dot=_dot):
    return dot(a.astype(BF16), b.astype(BF16))


def _dot_exact_lhs(m01, b, terms=3):
    acc = None
    for _ in range(terms):
        piece = b.astype(BF16)
        b = b - piece.astype(F32)
        part = _dot(m01, piece)
        acc = part if acc is None else acc + part
    return acc


def _dot_exact_rhs(a, m_twice):
    hi, lo = _split2(a)
    return _dot(jnp.concatenate([hi, lo], axis=1), m_twice)


def _softplus(z):
    return jnp.maximum(z, 0.0) + jnp.log1p(jnp.exp(-jnp.abs(z)))


def _softplus_fast(z):
    return jnp.maximum(z, 0.0) + jnp.log(1.0 + jnp.exp(-jnp.abs(z)))


def _silu(x):
    return x * jax.nn.sigmoid(x)


def _rms(y, g):
    ms = jnp.mean(y * y, axis=-1, keepdims=True)
    return y * lax.rsqrt(ms + EPS) * g


def _rmsnorm_kernel(x_ref, g_ref, o_ref):
    o_ref[...] = _rms(x_ref[...], g_ref[...]).astype(o_ref.dtype)


def _rmsnorm_bf16(x, g):
    t, d = x.shape
    tm = min(t, 512)
    return pl.pallas_call(
        _rmsnorm_kernel,
        grid=(t // tm,),
        in_specs=[pl.BlockSpec((tm, d), lambda i: (i, 0)), pl.BlockSpec((1, d), lambda i: (0, 0))],
        out_specs=pl.BlockSpec((tm, d), lambda i: (i, 0)),
        out_shape=jax.ShapeDtypeStruct((t, d), BF16),
        compiler_params=_cparams(("parallel",), 32),
        name="rmsnorm",
    )(x, g.reshape(1, d))


def _proj_kernel(x_ref, w_ref, g_ref, o_ref, *, head_norm):
    acc = _dot(x_ref[...], w_ref[...].astype(BF16))
    if head_norm:
        g = g_ref[...]
        for s in range(acc.shape[1] // HEAD):
            sl = slice(s * HEAD, (s + 1) * HEAD)
            o_ref[:, sl] = _rms(acc[:, sl], g).astype(o_ref.dtype)
    else:
        o_ref[...] = acc.astype(o_ref.dtype)


def _proj(h, w, col0, n_cols, out_dtype, head_gain=None):
    t, k = h.shape
    tm = min(t, 2048)
    tn = min(512, n_cols)
    assert col0 % tn == 0 and n_cols % tn == 0 and t % tm == 0
    cb = col0 // tn
    g = jnp.ones((1, HEAD), F32) if head_gain is None else head_gain.reshape(1, HEAD).astype(F32)
    return pl.pallas_call(
        functools.partial(_proj_kernel, head_norm=head_gain is not None),
        grid=(t // tm, n_cols // tn),
        in_specs=[pl.BlockSpec((tm, k), lambda i, j: (i, 0)),
                  pl.BlockSpec((k, tn), lambda i, j: (0, cb + j)),
                  pl.BlockSpec((1, HEAD), lambda i, j: (0, 0))],
        out_specs=pl.BlockSpec((tm, tn), lambda i, j: (i, j)),
        out_shape=jax.ShapeDtypeStruct((t, n_cols), out_dtype),
        compiler_params=_cparams(("parallel", "arbitrary"), 48),
        name="proj",
    )(h, w, g)


def _in_proj_kernel(h_ref, wm_ref, wz_ref, wl_ref, qg_ref, kg_ref,
                    q_ref, k_ref, v_ref, raw_ref, gz_ref, ab_ref, *, edges):
    j = pl.program_id(1)

    def head_normed(acc, g_ref):
        g = g_ref[...]
        return jnp.concatenate(
            [_rms(acc[:, s * HEAD:(s + 1) * HEAD], g) for s in range(acc.shape[1] // HEAD)], axis=1)

    def group(w_ref, out_ref, g_ref=None):
        def run():
            acc = _dot_nt(h_ref[...], w_ref[...].astype(BF16))
            out_ref[...] = (acc if g_ref is None else head_normed(acc, g_ref)).astype(out_ref.dtype)
        return run

    runs = (group(wm_ref, q_ref, qg_ref), group(wm_ref, k_ref, kg_ref), group(wm_ref, v_ref),
            group(wm_ref, raw_ref), group(wz_ref, gz_ref), group(wl_ref, ab_ref))
    for lo, hi, run in zip((0,) + edges[:-1], edges, runs):
        pl.when((j >= lo) & (j < hi))(run)


def _in_proj(h, w_t, q_gain, k_gain, q_dtype, main_widths, gate_width, n_logit):
    t, k = h.shape
    tm = min(t, 2048)
    tn = 256 if t > 1024 else 512
    n_main = sum(main_widths)
    widths = tuple(main_widths) + (gate_width, tn)
    assert all(w % tn == 0 for w in widths) and t % tm == 0 and n_logit <= tn
    nblk = [w // tn for w in widths]
    edges = tuple(int(sum(nblk[:i + 1])) for i in range(len(nblk)))
    starts = (0,) + edges[:-1]
    w_gate = w_t[n_main + n_logit:n_main + n_logit + gate_width]
    w_logit = jnp.pad(w_t[n_main:n_main + n_logit], ((0, tn - n_logit), (0, 0)))

    def swept(lo, n):
        return lambda i, j: (i, jnp.clip(j - lo, 0, n - 1))

    n_main_blk, gate_lo, n_gate_blk = edges[len(main_widths) - 1], starts[-2], nblk[-2]
    dtypes = (q_dtype,) + (F32,) * (len(widths) - 1)
    gain = lambda g: g.reshape(1, HEAD).astype(F32)
    return pl.pallas_call(
        functools.partial(_in_proj_kernel, edges=edges),
        grid=(t // tm, edges[-1]),
        in_specs=[pl.BlockSpec((tm, k), lambda i, j: (i, 0)),
                  pl.BlockSpec((tn, k), lambda i, j: (jnp.minimum(j, n_main_blk - 1), 0)),
                  pl.BlockSpec((tn, k), lambda i, j: (jnp.clip(j - gate_lo, 0, n_gate_blk - 1), 0)),
                  pl.BlockSpec((tn, k), lambda i, j: (0, 0), pipeline_mode=pl.Buffered(1)),
                  pl.BlockSpec((1, HEAD), lambda i, j: (0, 0)),
                  pl.BlockSpec((1, HEAD), lambda i, j: (0, 0))],
        out_specs=[pl.BlockSpec((tm, tn), swept(lo, n)) for lo, n in zip(starts, nblk)],
        out_shape=[jax.ShapeDtypeStruct((t, w), dt) for w, dt in zip(widths, dtypes)],
        compiler_params=_cparams(("parallel", "arbitrary"), 56),
        name="in_proj",
    )(h, w_t, w_gate, w_logit, gain(q_gain), gain(k_gain))


def _minus_later_twice(n):
    row = lax.broadcasted_iota(jnp.int32, (n, n), 0)
    col = lax.broadcasted_iota(jnp.int32, (n, n), 1)
    m = jnp.where(row > col, -1.0, 0.0).astype(BF16)
    return jnp.concatenate([m, m], axis=0)


def _sb_prompt_kernel(bias_ref, q_ref, k_ref, v_ref, o_ref, *, tq, hpb):
    hb = pl.program_id(1)
    i = pl.program_id(2)
    row = lax.broadcasted_iota(jnp.int32, (tq, tq), 0)
    col = lax.broadcasted_iota(jnp.int32, (tq, tq), 1)
    minus_later = _minus_later_twice(tq)
    causal = col < row
    heads = range(hpb)
    qs = [q_ref[0, :, g * HEAD:(g + 1) * HEAD] for g in heads]
    biases = [bias_ref[hb * hpb + g] for g in heads]

    def block(j, carry, masked):
        start = pl.multiple_of(j * tq, tq)
        kb = k_ref[0, pl.ds(start, tq), :].astype(BF16)
        vb = v_ref[0, pl.ds(start, tq), :].astype(BF16)
        zs = [_dot_nt(qs[g], kb[:, g * HEAD:(g + 1) * HEAD]) + biases[g] for g in heads]
        sps = [_softplus_fast(z) for z in zs]
        leaves = [jnp.where(causal, sp, 0.0) if masked else sp for sp in sps]
        survs = [_dot_exact_rhs(lv, minus_later) for lv in leaves]
        out = []
        for g in heads:
            csum, acc = carry[g]
            w = jnp.exp(zs[g] - sps[g] + survs[g] + csum)
            if masked:
                w = jnp.where(causal, w, 0.0)
            acc = acc + _dot(w.astype(BF16), vb[:, g * HEAD:(g + 1) * HEAD])
            out.append((csum - jnp.sum(leaves[g], axis=-1, keepdims=True), acc))
        return tuple(out)

    init = tuple((jnp.zeros((tq, 1), F32), jnp.zeros((tq, HEAD), F32)) for _ in heads)
    carry = block(i, init, True)
    carry = lax.fori_loop(0, i, lambda jj, c: block(i - 1 - jj, c, False), carry)
    for g in heads:
        o_ref[0, :, g * HEAD:(g + 1) * HEAD] = carry[g][1].astype(o_ref.dtype)


def _sb_prompt(q, k, v, bias, b, s):
    width = q.shape[1]
    nh = width // HEAD
    hpb = 4
    tq = min(s, 256)
    assert s % tq == 0 and nh % hpb == 0
    bw = hpb * HEAD
    q3, k3, v3 = (a.reshape(b, s, width) for a in (q, k, v))
    out = pl.pallas_call(
        functools.partial(_sb_prompt_kernel, tq=tq, hpb=hpb),
        grid=(b, nh // hpb, s // tq),
        in_specs=[pl.BlockSpec(memory_space=pltpu.SMEM),
                  pl.BlockSpec((1, tq, bw), lambda bi, h, i: (bi, i, h)),
                  pl.BlockSpec((1, s, bw), lambda bi, h, i: (bi, 0, h)),
                  pl.BlockSpec((1, s, bw), lambda bi, h, i: (bi, 0, h))],
        out_specs=pl.BlockSpec((1, tq, bw), lambda bi, h, i: (bi, i, h)),
        out_shape=jax.ShapeDtypeStruct((b, s, width), BF16),
        compiler_params=_cparams(("parallel", "parallel", "arbitrary"), 32),
        name="sb_prompt",
    )(bias.astype(F32), q3, k3, v3)
    return out.reshape(b * s, width)


def _sb_sample_kernel(pt_ref, bias_ref, q_ref, knew_ref, vnew_ref, *refs, n_pages, n_new):
    del pt_ref
    k_pages = refs[:n_pages]
    v_pages = refs[n_pages:2 * n_pages]
    o_ref = refs[2 * n_pages]
    rows = q_ref.shape[1]
    nh = o_ref.shape[2]
    q = q_ref[0].astype(BF16)
    bias = bias_ref[...]
    minus_later = _minus_later_twice(HEAD)

    def own_head(width):
        r = lax.broadcasted_iota(jnp.int32, (rows, width), 0)
        c = lax.broadcasted_iota(jnp.int32, (rows, width), 1)
        return r, c, (r & (nh - 1)) == (c & (nh - 1))

    r, c, own = own_head(knew_ref.shape[1])
    shift = int(math.log2(nh))
    new_mask = own & ((c >> shift) < (r >> shift)) & ((c >> shift) < n_new)
    page_mask = own_head(k_pages[0].shape[2])[2]
    segs = [(lambda: knew_ref[0], lambda: vnew_ref[0], new_mask)]
    segs += [(lambda p=p: k_pages[p][0, 0], lambda p=p: v_pages[p][0, 0], page_mask)
             for p in reversed(range(n_pages))]
    n = len(segs)
    zs, sps, leaves, sums = [None] * n, [None] * n, [None] * n, [None] * n
    csum = jnp.zeros((rows, 1), F32)
    acc = jnp.zeros((rows, HEAD), F32)
    for step in range(n + 2):
        if step < n:
            zs[step] = _dot_nt(q, segs[step][0]().astype(BF16)) + bias
        i = step - 1
        if 0 <= i < n:
            sps[i] = _softplus_fast(zs[i])
            leaves[i] = jnp.where(segs[i][2], sps[i], 0.0)
            sums[i] = [_dot_exact_rhs(leaves[i][:, blk * HEAD:(blk + 1) * HEAD], minus_later)
                       for blk in range(zs[i].shape[1] // HEAD)]
        i = step - 2
        if 0 <= i < n:
            pieces = [None] * len(sums[i])
            for blk in reversed(range(len(sums[i]))):
                pieces[blk] = sums[i][blk] + csum
                csum = csum - jnp.sum(leaves[i][:, blk * HEAD:(blk + 1) * HEAD], axis=-1, keepdims=True)
            survive = pieces[0] if len(pieces) == 1 else jnp.concatenate(pieces, axis=1)
            w = jnp.where(segs[i][2], jnp.exp(zs[i] - sps[i] + survive), 0.0)
            acc = acc + _dot(w.astype(BF16), segs[i][1]().astype(BF16))
            zs[i] = sps[i] = leaves[i] = sums[i] = None
    o_ref[0] = acc.reshape(rows // nh, nh, HEAD)


def _sb_sample(q, k_new, v_new, bias, cache_k, cache_v, layer, page_table, b, s):
    width = q.shape[1]
    nh = width // HEAD
    n_pages = page_table.shape[1]
    rows = s * nh
    new_rows = HEAD
    assert rows <= new_rows and cache_k.shape[2] == PAGE and nh & (nh - 1) == 0

    key_rows = lambda a: jnp.pad(a.reshape(b, rows, HEAD), ((0, 0), (0, new_rows - rows), (0, 0)))
    flat = lambda c: c.reshape(c.shape[0], c.shape[1], PAGE * nh, HEAD)
    bias_rows = jnp.tile(bias.astype(F32), s).reshape(rows, 1)

    def page_spec(p):
        return pl.BlockSpec((1, 1, PAGE * nh, HEAD), lambda bi, pt: (layer, pt[bi, p], 0, 0))

    grid_spec = pltpu.PrefetchScalarGridSpec(
        num_scalar_prefetch=1,
        grid=(b,),
        in_specs=[pl.BlockSpec((rows, 1), lambda bi, pt: (0, 0)),
                  pl.BlockSpec((1, rows, HEAD), lambda bi, pt: (bi, 0, 0)),
                  pl.BlockSpec((1, new_rows, HEAD), lambda bi, pt: (bi, 0, 0)),
                  pl.BlockSpec((1, new_rows, HEAD), lambda bi, pt: (bi, 0, 0))]
        + [page_spec(p) for p in range(n_pages)] * 2,
        out_specs=pl.BlockSpec((1, s, nh, HEAD), lambda bi, pt: (bi, 0, 0, 0)),
    )
    out = pl.pallas_call(
        functools.partial(_sb_sample_kernel, n_pages=n_pages, n_new=s),
        grid_spec=grid_spec,
        out_shape=jax.ShapeDtypeStruct((b, s, nh, HEAD), F32),
        compiler_params=_cparams(("arbitrary",), 56),
        name="sb_sample",
    )(page_table, bias_rows, q.reshape(b, rows, HEAD), key_rows(k_new), key_rows(v_new),
      *([flat(cache_k)] * n_pages), *([flat(cache_v)] * n_pages))
    return out.reshape(b * s, width)


def _gdn_kernel(raw_ref, ab_ref, gz_ref, convw_ref, conv0_ref, s0_ref, alog_ref, dtb_ref, gn_ref,
                o_ref, s_ref, pad_ref, *, c, n_valid):
    n = pl.program_id(1)
    bb, nh = s_ref.shape[0], s_ref.shape[1]
    qkw = nh * HEAD

    @pl.when(n == 0)
    def _():
        for bi in range(bb):
            pad_ref[bi, 0:8, :] = jnp.zeros((8, pad_ref.shape[2]), F32)
            pad_ref[bi, 5:8, :] = conv0_ref[bi]
        s_ref[...] = s0_ref[...]

    cw = convw_ref[...]
    rc = lax.broadcasted_iota(jnp.int32, (c, c), 0)
    cc = lax.broadcasted_iota(jnp.int32, (c, c), 1)
    upto_c = (cc <= rc).astype(BF16)
    gn = gn_ref[...]

    hpg = min(nh, max(1, GDN_CHUNK // c))
    rows = hpg * c

    def stack(pieces):
        return pieces[0] if len(pieces) == 1 else jnp.concatenate(pieces, axis=0)

    def unit(x):
        return x * lax.rsqrt(jnp.sum(x * x, axis=-1, keepdims=True) + EPS)

    groups, ops = [], []
    for bi in range(bb):
        pad_ref[bi, 8:8 + c, :] = raw_ref[bi]
        conv = pad_ref[bi, 5:5 + c, :] * cw[0:1, :]
        for j in range(1, GDN_CONV):
            conv = conv + pad_ref[bi, 5 + j:5 + j + c, :] * cw[j:j + 1, :]
        pad_ref[bi, 5:8, :] = pad_ref[bi, 5 + c:8 + c, :]
        xs = _silu(conv)

        ab = ab_ref[bi]
        live = lax.broadcasted_iota(jnp.int32, ab.shape, 0) < n_valid
        g_all = jnp.where(live, -jnp.exp(alog_ref[...]) * _softplus(ab + dtb_ref[...]), 0.0)
        beta_all = jnp.where(live, jax.nn.sigmoid(ab), 0.0)
        cum_all = _dot_exact_lhs(upto_c, g_all)
        cum_last = cum_all[c - 1:c, :]
        ecum_all = jnp.exp(cum_all)
        eend_all = jnp.exp(cum_last - cum_all)
        glast_all = jnp.exp(cum_last)
        for g0 in range(0, nh, hpg):
            hs = list(range(g0, g0 + hpg))
            q = unit(stack([xs[:, h * HEAD:(h + 1) * HEAD] for h in hs])) * (HEAD ** -0.5)
            k = unit(stack([xs[:, qkw + h * HEAD:qkw + (h + 1) * HEAD] for h in hs]))
            v = stack([xs[:, 2 * qkw + h * HEAD:2 * qkw + (h + 1) * HEAD] for h in hs])
            col = lambda a, off: stack([a[:, off + h:off + h + 1] for h in hs])
            groups.append((bi, hs))
            ops.append(dict(q=q, k=k, v=v, g=col(g_all, 0), beta=col(beta_all, nh),
                            ecum=col(ecum_all, 0), eend=col(eend_all, 0), glast=glast_all))

    ri = lax.broadcasted_iota(jnp.int32, (rows, rows), 0)
    ci = lax.broadcasted_iota(jnp.int32, (rows, rows), 1)
    shift = int(math.log2(c))
    same = (ri >> shift) == (ci >> shift)
    strict = same & (ri > ci)
    incl = same & (ri >= ci)
    upto = incl.astype(BF16)
    eye = (ri == ci).astype(F32)
    diffs = [_dot_exact_lhs(upto, jnp.where(strict, jnp.broadcast_to(o["g"], (rows, rows)), 0.0), terms=2)
             for o in ops]
    kks = [_dot_bf(o["k"], o["k"], _dot_nt) for o in ops]
    qks = [_dot_bf(o["q"], o["k"], _dot_nt) for o in ops]
    decays = [jnp.where(incl, jnp.exp(d), 0.0) for d in diffs]
    npows = [jnp.where(strict, o["beta"] * kk * dec, 0.0) for o, kk, dec in zip(ops, kks, decays)]
    invs = [eye - n for n in npows]
    for _ in range(shift - 1):
        npows = [_dot_hp(n, n) for n in npows]
        invs = [inv + _dot_hp(inv, n) for inv, n in zip(invs, npows)]
    sols = [_dot_bf(inv, jnp.concatenate([o["v"] * o["beta"], o["k"] * (o["beta"] * o["ecum"])], axis=-1))
            for inv, o in zip(invs, ops)]
    qdecs = [o["q"] * o["ecum"] for o in ops]
    from_state = []
    for (bi, hs), sol, qdec in zip(groups, sols, qdecs):
        for j, h in enumerate(hs):
            r = slice(j * c, (j + 1) * c)
            from_state.append(_dot_bf(jnp.concatenate([sol[r, HEAD:], qdec[r, :]], axis=0), s_ref[bi, h]))
    v_news, outs = [], []
    for gi, (sol, qk, dec) in enumerate(zip(sols, qks, decays)):
        mine = from_state[gi * hpg:(gi + 1) * hpg]
        v_new = sol[:, :HEAD] - stack([m[:c, :] for m in mine])
        v_news.append(v_new)
        outs.append(stack([m[c:, :] for m in mine]) + _dot_bf(qk * dec, v_new))
    for (bi, hs), o, v_new in zip(groups, ops, v_news):
        k_end = o["k"] * o["eend"]
        for j, h in enumerate(hs):
            r = slice(j * c, (j + 1) * c)
            s_ref[bi, h] = (o["glast"][:, h:h + 1] * s_ref[bi, h]
                            + _dot_bf(k_end[r, :], v_new[r, :], _dot_tn))
    for (bi, hs), out in zip(groups, outs):
        normed = _rms(out, gn)
        for j, h in enumerate(hs):
            sl = slice(h * HEAD, (h + 1) * HEAD)
            o_ref[bi, :, sl] = (normed[j * c:(j + 1) * c, :] * _silu(gz_ref[bi, :, sl])).astype(o_ref.dtype)


def _gdn(raw, ab, gz, conv_w, conv0, s0, a_log, dt_bias, out_gain, b, s, c, n_valid):
    nh = s0.shape[1]
    ch = raw.shape[2]
    bb = 2 if (s == c and c * nh <= GDN_CHUNK and b % 2 == 0) else 1
    pad_lane = lambda a: jnp.pad(a.astype(F32), (0, HEAD - a.shape[0])).reshape(1, HEAD)
    out, state = pl.pallas_call(
        functools.partial(_gdn_kernel, c=c, n_valid=n_valid),
        grid=(b // bb, s // c),
        in_specs=[pl.BlockSpec((bb, c, ch), lambda bi, n: (bi, n, 0)),
                  pl.BlockSpec((bb, c, HEAD), lambda bi, n: (bi, n, 0)),
                  pl.BlockSpec((bb, c, nh * HEAD), lambda bi, n: (bi, n, 0)),
                  pl.BlockSpec((GDN_CONV, ch), lambda bi, n: (0, 0)),
                  pl.BlockSpec((bb, GDN_CONV - 1, ch), lambda bi, n: (bi, 0, 0)),
                  pl.BlockSpec((bb, nh, HEAD, HEAD), lambda bi, n: (bi, 0, 0, 0)),
                  pl.BlockSpec((1, HEAD), lambda bi, n: (0, 0)),
                  pl.BlockSpec((1, HEAD), lambda bi, n: (0, 0)),
                  pl.BlockSpec((1, HEAD), lambda bi, n: (0, 0))],
        out_specs=[pl.BlockSpec((bb, c, nh * HEAD), lambda bi, n: (bi, n, 0)),
                   pl.BlockSpec((bb, nh, HEAD, HEAD), lambda bi, n: (bi, 0, 0, 0))],
        out_shape=[jax.ShapeDtypeStruct((b, s, nh * HEAD), BF16),
                   jax.ShapeDtypeStruct(s0.shape, F32)],
        scratch_shapes=[pltpu.VMEM((bb, 8 + c, ch), F32)],
        compiler_params=_cparams(("parallel", "arbitrary"), 32),
        name="gdn",
    )(raw, ab, gz, conv_w, conv0, s0, pad_lane(a_log), pad_lane(dt_bias), out_gain.reshape(1, HEAD))
    return out, state


def _mix_out_kernel(*refs, n_a):
    a_refs = refs[:n_a]
    w_ref, x_ref, g_ref, wq_ref, qg_ref, x_out, q_out = refs[n_a:]
    acc = x_ref[...]
    off = 0
    for a in a_refs:
        ka = a.shape[1]
        acc = acc + _dot(a[...].astype(BF16), w_ref[off:off + ka, :])
        off += ka
    x_out[...] = acc
    xq = _dot(_rms(acc, g_ref[...]).astype(BF16), wq_ref[...])
    qg = qg_ref[...]
    for s in range(xq.shape[1] // HEAD):
        sl = slice(s * HEAD, (s + 1) * HEAD)
        q_out[:, sl] = _rms(xq[:, sl], qg).astype(q_out.dtype)


def _mix_out(acts, w_bf16, x, next_gain, wq_bf16, q_head_gain):
    t, d = x.shape
    qw = wq_bf16.shape[1]
    tm = min(t, 256)
    return pl.pallas_call(
        functools.partial(_mix_out_kernel, n_a=len(acts)),
        grid=(t // tm,),
        in_specs=[pl.BlockSpec((tm, a.shape[1]), lambda i: (i, 0)) for a in acts]
        + [pl.BlockSpec(w_bf16.shape, lambda i: (0, 0)),
           pl.BlockSpec((tm, d), lambda i: (i, 0)),
           pl.BlockSpec((1, d), lambda i: (0, 0)),
           pl.BlockSpec(wq_bf16.shape, lambda i: (0, 0)),
           pl.BlockSpec((1, HEAD), lambda i: (0, 0))],
        out_specs=[pl.BlockSpec((tm, d), lambda i: (i, 0)), pl.BlockSpec((tm, qw), lambda i: (i, 0))],
        out_shape=[jax.ShapeDtypeStruct((t, d), F32), jax.ShapeDtypeStruct((t, qw), BF16)],
        compiler_params=_cparams(("parallel",), 48),
        name="mix_out",
    )(*acts, w_bf16, x, next_gain.reshape(1, d), wq_bf16, q_head_gain.reshape(1, HEAD))


def _xattn_prompt_kernel(q_ref, k_ref, v_ref, o_ref, *, scale):
    for h in range(q_ref.shape[2] // HEAD):
        sl = slice(h * HEAD, (h + 1) * HEAD)
        logits = _dot_nt(q_ref[0, :, sl], k_ref[0, :, sl].astype(BF16)) * scale
        p = jnp.exp(logits - jnp.max(logits, axis=-1, keepdims=True))
        attn = p / jnp.sum(p, axis=-1, keepdims=True)
        o_ref[0, :, sl] = _dot(attn.astype(BF16), v_ref[0, :, sl].astype(BF16)).astype(o_ref.dtype)


def _xattn_prompt(xq, mem_k, mem_v, b, s):
    width = xq.shape[1]
    m = mem_k.shape[1]
    tq = min(s, 512)
    out = pl.pallas_call(
        functools.partial(_xattn_prompt_kernel, scale=HEAD ** -0.5),
        grid=(b, s // tq),
        in_specs=[pl.BlockSpec((1, tq, width), lambda bi, i: (bi, i, 0)),
                  pl.BlockSpec((1, m, width), lambda bi, i: (bi, 0, 0)),
                  pl.BlockSpec((1, m, width), lambda bi, i: (bi, 0, 0))],
        out_specs=pl.BlockSpec((1, tq, width), lambda bi, i: (bi, i, 0)),
        out_shape=jax.ShapeDtypeStruct((b, s, width), BF16),
        compiler_params=_cparams(("parallel", "arbitrary"), 32),
        name="xattn_prompt",
    )(xq.reshape(b, s, width), mem_k, mem_v)
    return out.reshape(b * s, width)


def _xattn_sample_kernel(q_ref, k_ref, v_ref, o_ref, *, n_new, scale):
    bb, nh = q_ref.shape[0], q_ref.shape[1]
    m = k_ref.shape[2] // nh
    pairs = [(i, h) for i in range(bb) for h in range(nh)]
    logits = [_dot_nt(q_ref[i, h].astype(BF16), k_ref[0, i, pl.ds(h, m, stride=nh), :].astype(BF16)) * scale
              for i, h in pairs]
    ps = [jnp.exp(lg - jnp.max(lg, axis=-1, keepdims=True)) for lg in logits]
    attns = [(p / jnp.sum(p, axis=-1, keepdims=True)).astype(BF16) for p in ps]
    for (i, h), attn in zip(pairs, attns):
        o_h = _dot(attn, v_ref[0, i, pl.ds(h, m, stride=nh), :].astype(BF16))
        o_ref[i, :, h * HEAD:(h + 1) * HEAD] = o_h[0:n_new, :]


def _xattn_sample(xq, mem_k, mem_v, layer, b, s):
    width = xq.shape[1]
    nh = width // HEAD
    m = mem_k.shape[2]
    qrows = 8
    bb = math.gcd(b, 4)
    q4 = jnp.pad(xq.reshape(b, s, nh, HEAD).transpose(0, 2, 1, 3), ((0, 0), (0, 0), (0, qrows - s), (0, 0)))
    mem_spec = pl.BlockSpec((1, bb, m * nh, HEAD), lambda bi: (layer, bi, 0, 0))
    flat = lambda c: c.reshape(c.shape[0], c.shape[1], m * nh, HEAD)
    out = pl.pallas_call(
        functools.partial(_xattn_sample_kernel, n_new=s, scale=HEAD ** -0.5),
        grid=(b // bb,),
        in_specs=[pl.BlockSpec((bb, nh, qrows, HEAD), lambda bi: (bi, 0, 0, 0)), mem_spec, mem_spec],
        out_specs=pl.BlockSpec((bb, s, width), lambda bi: (bi, 0, 0)),
        out_shape=jax.ShapeDtypeStruct((b, s, width), F32),
        compiler_params=_cparams(("parallel",), 32),
        name="xattn_sample",
    )(q4, flat(mem_k), flat(mem_v))
    return out.reshape(b * s, width)


def _ffn_kernel(x_ref, a_ref, wa_ref, g_ref, wg_ref, wu_ref, wd_ref, o_ref, h_ref):
    j = pl.program_id(1)

    @pl.when(j == 0)
    def _():
        x = x_ref[...] + _dot(a_ref[...].astype(BF16), wa_ref[...])
        h_ref[...] = _rms(x, g_ref[...]).astype(h_ref.dtype)
        o_ref[...] = x

    h = h_ref[...]
    gate = _dot(h, wg_ref[...].astype(BF16))
    up = _dot(h, wu_ref[...].astype(BF16))
    act = (_silu(gate) * up).astype(BF16)
    o_ref[...] += _dot(act, wd_ref[...].astype(BF16))


def _ffn(x, attn, w_attn_bf16, gain, w_gate_up, w_down):
    t, d = x.shape
    f = w_down.shape[0]
    tm = min(t, 1024)
    tf = 256
    assert f % tf == 0 and t % tm == 0
    nf = f // tf
    return pl.pallas_call(
        _ffn_kernel,
        grid=(t // tm, nf),
        in_specs=[pl.BlockSpec((tm, d), lambda i, j: (i, 0)),
                  pl.BlockSpec((tm, attn.shape[1]), lambda i, j: (i, 0)),
                  pl.BlockSpec(w_attn_bf16.shape, lambda i, j: (0, 0)),
                  pl.BlockSpec((1, d), lambda i, j: (0, 0)),
                  pl.BlockSpec((d, tf), lambda i, j: (0, j)),
                  pl.BlockSpec((d, tf), lambda i, j: (0, nf + j)),
                  pl.BlockSpec((tf, d), lambda i, j: (j, 0))],
        out_specs=pl.BlockSpec((tm, d), lambda i, j: (i, 0)),
        out_shape=jax.ShapeDtypeStruct((t, d), F32),
        scratch_shapes=[pltpu.VMEM((tm, d), BF16)],
        compiler_params=_cparams(("parallel", "arbitrary"), 60),
        name="ffn",
    )(x, attn, w_attn_bf16, gain.reshape(1, d), w_gate_up, w_gate_up, w_down)


def _decoder_layer(x, b, s, p, sb_attend, xattend, conv0, s0):
    d = x.shape[1]
    sbw = SB_HEADS * HEAD
    gw = GDN_HEADS * HEAD
    h = _rmsnorm_bf16(x, p["norm_mix_g"])
    q, k_new, v_new, raw, gz, ab = _in_proj(
        h, p["w_in_t"], p["sb_q_norm_g"] * (HEAD ** -0.5), p["sb_k_norm_g"], p["sb_q_dtype"],
        (sbw, sbw, sbw, 3 * gw), gw, 2 * GDN_HEADS)
    o_sb = sb_attend(q, k_new, v_new)

    c = math.gcd(s, GDN_CHUNK)
    raw3, ab3, gz3 = raw.reshape(b, s, 3 * gw), ab.reshape(b, s, ab.shape[1]), gz.reshape(b, s, gw)
    conv_new = raw3[:, s - (GDN_CONV - 1):, :]
    n_valid = c
    if c < 8:
        padn = ((0, 0), (0, 8 - s), (0, 0))
        raw3, ab3, gz3 = jnp.pad(raw3, padn), jnp.pad(ab3, padn), jnp.pad(gz3, padn)
        c = 8
    o_gdn, state = _gdn(raw3, ab3, gz3, p["gdn_conv_w"], conv0, s0, p["gdn_a_log"], p["gdn_dt_bias"],
                        p["gdn_out_norm_g"], b, raw3.shape[1], c, n_valid)
    o_gdn = o_gdn[:, :s, :].reshape(b * s, gw)

    x1, xq = _mix_out([o_sb, o_gdn], p["w_out_bf"], x, p["norm_x_g"], p["w_xq_bf"], p["x_q_norm_g"])
    xo = xattend(xq)
    y = _ffn(x1, xo, p["w_xo_bf"], p["norm_ffn_g"], p["w_gate_up"], p["w_down"])
    return y, k_new, v_new, state, conv_new


def kernel(x_prompt, x_sample, mem_prompt, cache_sb_k, cache_sb_v, page_table, state_gdn, state_gdn_conv, cache_mem_k, cache_mem_v, norm_mix_g, w_in, sb_q_norm_g, sb_k_norm_g, sb_logit_bias, gdn_conv_w, gdn_a_log, gdn_dt_bias, gdn_out_norm_g, w_out, norm_x_g, norm_mem_g, w_xq, w_mk, w_mv, x_q_norm_g, x_k_norm_g, w_xo, norm_ffn_g, w_gate_up, w_down):
    bp, sp_, d = x_prompt.shape
    bs, ss, _ = x_sample.shape
    depth = w_in.shape[0]
    m = mem_prompt.shape[1]
    sbw = SB_HEADS * HEAD
    gw = GDN_HEADS * HEAD
    xw = X_HEADS * HEAD
    ch = 3 * gw
    ab0 = 3 * sbw + ch
    yp = x_prompt.reshape(bp * sp_, d)
    ys = x_sample.reshape(bs * ss, d)
    outs = [[] for _ in range(10)]
    for l in range(depth):
        p = {
            "norm_mix_g": norm_mix_g[l], "w_in_t": jnp.swapaxes(w_in[l], 0, 1), "sb_q_norm_g": sb_q_norm_g[l],
            "sb_k_norm_g": sb_k_norm_g[l],
            "gdn_conv_w": gdn_conv_w[l], "gdn_a_log": gdn_a_log[l], "gdn_dt_bias": gdn_dt_bias[l],
            "gdn_out_norm_g": gdn_out_norm_g[l], "norm_x_g": norm_x_g[l], "x_q_norm_g": x_q_norm_g[l],
            "norm_ffn_g": norm_ffn_g[l], "w_gate_up": w_gate_up[l], "w_down": w_down[l],
            "w_out_bf": w_out[l].astype(BF16), "w_xq_bf": w_xq[l].astype(BF16), "w_xo_bf": w_xo[l].astype(BF16),
        }
        bias = sb_logit_bias[l]
        hm = _rmsnorm_bf16(mem_prompt.reshape(bp * m, d), norm_mem_g[l])
        mem_k = _proj(hm, w_mk[l], 0, xw, F32, x_k_norm_g[l]).reshape(bp, m, xw)
        mem_v = _proj(hm, w_mv[l], 0, xw, F32).reshape(bp, m, xw)

        pp = dict(p, sb_q_dtype=BF16)
        yp, kp, vp, stp, cvp = _decoder_layer(
            yp, bp, sp_, pp,
            lambda q, k, v: _sb_prompt(q, k, v, bias, bp, sp_),
            lambda xq: _xattn_prompt(xq, mem_k, mem_v, bp, sp_),
            jnp.zeros((bp, GDN_CONV - 1, ch), F32), jnp.zeros((bp, GDN_HEADS, HEAD, HEAD), F32))

        ps = dict(p, sb_q_dtype=F32)
        ys, ks, vs, sts, cvs = _decoder_layer(
            ys, bs, ss, ps,
            lambda q, k, v: _sb_sample(q, k, v, bias, cache_sb_k, cache_sb_v, l, page_table, bs, ss),
            lambda xq: _xattn_sample(xq, cache_mem_k, cache_mem_v, l, bs, ss),
            state_gdn_conv[l], state_gdn[l])

        for acc, val in zip(outs, (
                kp.reshape(bp, sp_, SB_HEADS, HEAD), vp.reshape(bp, sp_, SB_HEADS, HEAD),
                ks.reshape(bs, ss, SB_HEADS, HEAD), vs.reshape(bs, ss, SB_HEADS, HEAD),
                stp.astype(state_gdn.dtype), cvp, sts.astype(state_gdn.dtype), cvs,
                mem_k.reshape(bp, m, X_HEADS, HEAD), mem_v.reshape(bp, m, X_HEADS, HEAD))):
            acc.append(val)
    return (yp.reshape(bp, sp_, d), ys.reshape(bs, ss, d)) + tuple(jnp.stack(o) for o in outs)
```

```python
import functools
import math

import jax
import jax.numpy as jnp
from jax import lax
from jax.experimental import pallas as pl
from jax.experimental.pallas import tpu as pltpu

F32 = jnp.float32
BF16 = jnp.bfloat16
EPS = 1e-6
HEAD = 128
SB_HEADS = 8
GDN_HEADS = 8
X_HEADS = 4
PAGE = 128
GDN_CONV = 4
GDN_CHUNK = 64
MIB = 1 << 20


def _cparams(sem, vmem_mib):
    return pltpu.CompilerParams(dimension_semantics=sem, vmem_limit_bytes=vmem_mib * MIB)


def _dot(a, b):
    return jnp.dot(a, b, preferred_element_type=F32)


def _dot_nt(a, b):
    return lax.dot_general(a, b, (((1,), (1,)), ((), ())), preferred_element_type=F32)


def _dot_tn(a, b):
    return lax.dot_general(a, b, (((0,), (0,)), ((), ())), preferred_element_type=F32)


def _split2(a):
    hi = a.astype(BF16)
    lo = (a - hi.astype(F32)).astype(BF16)
    return hi, lo


def _dot_hp(a, b, dot=_dot):
    ah, al = _split2(a)
    bh, bl = _split2(b)
    return dot(ah, bh) + (dot(ah, bl) + dot(al, bh))


def _dot_bf(a, b, dot=_dot):
    return dot(a.astype(BF16), b.astype(BF16))


def _dot_exact_lhs(m01, b, terms=3):
    acc = None
    for _ in range(terms):
        piece = b.astype(BF16)
        b = b - piece.astype(F32)
        part = _dot(m01, piece)
        acc = part if acc is None else acc + part
    return acc


def _dot_exact_rhs(a, m_twice):
    hi, lo = _split2(a)
    return _dot(jnp.concatenate([hi, lo], axis=1), m_twice)


def _softplus(z):
    return jnp.maximum(z, 0.0) + jnp.log1p(jnp.exp(-jnp.abs(z)))


def _softplus_fast(z):
    return jnp.maximum(z, 0.0) + jnp.log(1.0 + jnp.exp(-jnp.abs(z)))


def _silu(x):
    return x * jax.nn.sigmoid(x)


def _rms(y, g):
    ms = jnp.mean(y * y, axis=-1, keepdims=True)
    return y * lax.rsqrt(ms + EPS) * g


def _rmsnorm_kernel(x_ref, g_ref, o_ref):
    o_ref[...] = _rms(x_ref[...], g_ref[...]).astype(o_ref.dtype)


def _rmsnorm_bf16(x, g):
    t, d = x.shape
    tm = min(t, 512)
    return pl.pallas_call(
        _rmsnorm_kernel,
        grid=(t // tm,),
        in_specs=[pl.BlockSpec((tm, d), lambda i: (i, 0)), pl.BlockSpec((1, d), lambda i: (0, 0))],
        out_specs=pl.BlockSpec((tm, d), lambda i: (i, 0)),
        out_shape=jax.ShapeDtypeStruct((t, d), BF16),
        compiler_params=_cparams(("parallel",), 32),
        name="rmsnorm",
    )(x, g.reshape(1, d))


def _proj_kernel(x_ref, w_ref, g_ref, o_ref, *, head_norm):
    acc = _dot(x_ref[...], w_ref[...].astype(BF16))
    if head_norm:
        g = g_ref[...]
        for s in range(acc.shape[1] // HEAD):
            sl = slice(s * HEAD, (s + 1) * HEAD)
            o_ref[:, sl] = _rms(acc[:, sl], g).astype(o_ref.dtype)
    else:
        o_ref[...] = acc.astype(o_ref.dtype)


def _proj(h, w, col0, n_cols, out_dtype, head_gain=None):
    t, k = h.shape
    tm = min(t, 2048)
    tn = min(512, n_cols)
    assert col0 % tn == 0 and n_cols % tn == 0 and t % tm == 0
    cb = col0 // tn
    g = jnp.ones((1, HEAD), F32) if head_gain is None else head_gain.reshape(1, HEAD).astype(F32)
    return pl.pallas_call(
        functools.partial(_proj_kernel, head_norm=head_gain is not None),
        grid=(t // tm, n_cols // tn),
        in_specs=[pl.BlockSpec((tm, k), lambda i, j: (i, 0)),
                  pl.BlockSpec((k, tn), lambda i, j: (0, cb + j)),
                  pl.BlockSpec((1, HEAD), lambda i, j: (0, 0))],
        out_specs=pl.BlockSpec((tm, tn), lambda i, j: (i, j)),
        out_shape=jax.ShapeDtypeStruct((t, n_cols), out_dtype),
        compiler_params=_cparams(("parallel", "arbitrary"), 48),
        name="proj",
    )(h, w, g)


def _in_proj_kernel(h_ref, wm_ref, wz_ref, wl_ref, qg_ref, kg_ref,
                    q_ref, k_ref, v_ref, raw_ref, gz_ref, ab_ref, *, edges):
    j = pl.program_id(1)

    def head_normed(acc, g_ref):
        g = g_ref[...]
        return jnp.concatenate(
            [_rms(acc[:, s * HEAD:(s + 1) * HEAD], g) for s in range(acc.shape[1] // HEAD)], axis=1)

    def group(w_ref, out_ref, g_ref=None):
        def run():
            acc = _dot_nt(h_ref[...], w_ref[...].astype(BF16))
            out_ref[...] = (acc if g_ref is None else head_normed(acc, g_ref)).astype(out_ref.dtype)
        return run

    runs = (group(wm_ref, q_ref, qg_ref), group(wm_ref, k_ref, kg_ref), group(wm_ref, v_ref),
            group(wm_ref, raw_ref), group(wz_ref, gz_ref), group(wl_ref, ab_ref))
    for lo, hi, run in zip((0,) + edges[:-1], edges, runs):
        pl.when((j >= lo) & (j < hi))(run)


def _in_proj(h, w_t, q_gain, k_gain, q_dtype, main_widths, gate_width, n_logit):
    t, k = h.shape
    tm = min(t, 2048)
    tn = 256 if t > 1024 else 512
    n_main = sum(main_widths)
    widths = tuple(main_widths) + (gate_width, tn)
    assert all(w % tn == 0 for w in widths) and t % tm == 0 and n_logit <= tn
    nblk = [w // tn for w in widths]
    edges = tuple(int(sum(nblk[:i + 1])) for i in range(len(nblk)))
    starts = (0,) + edges[:-1]
    w_gate = w_t[n_main + n_logit:n_main + n_logit + gate_width]
    w_logit = jnp.pad(w_t[n_main:n_main + n_logit], ((0, tn - n_logit), (0, 0)))

    def swept(lo, n):
        return lambda i, j: (i, jnp.clip(j - lo, 0, n - 1))

    n_main_blk, gate_lo, n_gate_blk = edges[len(main_widths) - 1], starts[-2], nblk[-2]
    dtypes = (q_dtype,) + (F32,) * (len(widths) - 1)
    gain = lambda g: g.reshape(1, HEAD).astype(F32)
    return pl.pallas_call(
        functools.partial(_in_proj_kernel, edges=edges),
        grid=(t // tm, edges[-1]),
        in_specs=[pl.BlockSpec((tm, k), lambda i, j: (i, 0)),
                  pl.BlockSpec((tn, k), lambda i, j: (jnp.minimum(j, n_main_blk - 1), 0)),
                  pl.BlockSpec((tn, k), lambda i, j: (jnp.clip(j - gate_lo, 0, n_gate_blk - 1), 0)),
                  pl.BlockSpec((tn, k), lambda i, j: (0, 0), pipeline_mode=pl.Buffered(1)),
                  pl.BlockSpec((1, HEAD), lambda i, j: (0, 0)),
                  pl.BlockSpec((1, HEAD), lambda i, j: (0, 0))],
        out_specs=[pl.BlockSpec((tm, tn), swept(lo, n)) for lo, n in zip(starts, nblk)],
        out_shape=[jax.ShapeDtypeStruct((t, w), dt) for w, dt in zip(widths, dtypes)],
        compiler_params=_cparams(("parallel", "arbitrary"), 56),
        name="in_proj",
    )(h, w_t, w_gate, w_logit, gain(q_gain), gain(k_gain))


def _minus_later_twice(n):
    row = lax.broadcasted_iota(jnp.int32, (n, n), 0)
    col = lax.broadcasted_iota(jnp.int32, (n, n), 1)
    m = jnp.where(row > col, -1.0, 0.0).astype(BF16)
    return jnp.concatenate([m, m], axis=0)


def _sb_prompt_kernel(bias_ref, q_ref, k_ref, v_ref, o_ref, *, tq, hpb):
    hb = pl.program_id(1)
    i = pl.program_id(2)
    row = lax.broadcasted_iota(jnp.int32, (tq, tq), 0)
    col = lax.broadcasted_iota(jnp.int32, (tq, tq), 1)
    minus_later = _minus_later_twice(tq)
    causal = col < row
    heads = range(hpb)
    qs = [q_ref[0, :, g * HEAD:(g + 1) * HEAD] for g in heads]
    biases = [bias_ref[hb * hpb + g] for g in heads]

    def block(j, carry, masked):
        start = pl.multiple_of(j * tq, tq)
        kb = k_ref[0, pl.ds(start, tq), :].astype(BF16)
        vb = v_ref[0, pl.ds(start, tq), :].astype(BF16)
        zs = [_dot_nt(qs[g], kb[:, g * HEAD:(g + 1) * HEAD]) + biases[g] for g in heads]
        sps = [_softplus_fast(z) for z in zs]
        leaves = [jnp.where(causal, sp, 0.0) if masked else sp for sp in sps]
        survs = [_dot_exact_rhs(lv, minus_later) for lv in leaves]
        out = []
        for g in heads:
            csum, acc = carry[g]
            w = jnp.exp(zs[g] - sps[g] + survs[g] + csum)
            if masked:
                w = jnp.where(causal, w, 0.0)
            acc = acc + _dot(w.astype(BF16), vb[:, g * HEAD:(g + 1) * HEAD])
            out.append((csum - jnp.sum(leaves[g], axis=-1, keepdims=True), acc))
        return tuple(out)

    init = tuple((jnp.zeros((tq, 1), F32), jnp.zeros((tq, HEAD), F32)) for _ in heads)
    carry = block(i, init, True)
    carry = lax.fori_loop(0, i, lambda jj, c: block(i - 1 - jj, c, False), carry)
    for g in heads:
        o_ref[0, :, g * HEAD:(g + 1) * HEAD] = carry[g][1].astype(o_ref.dtype)


def _sb_prompt(q, k, v, bias, b, s):
    width = q.shape[1]
    nh = width // HEAD
    hpb = 8
    tq = min(s, 256)
    assert s % tq == 0 and nh % hpb == 0
    bw = hpb * HEAD
    q3, k3, v3 = (a.reshape(b, s, width) for a in (q, k, v))
    out = pl.pallas_call(
        functools.partial(_sb_prompt_kernel, tq=tq, hpb=hpb),
        grid=(b, nh // hpb, s // tq),
        in_specs=[pl.BlockSpec(memory_space=pltpu.SMEM),
                  pl.BlockSpec((1, tq, bw), lambda bi, h, i: (bi, i, h)),
                  pl.BlockSpec((1, s, bw), lambda bi, h, i: (bi, 0, h)),
                  pl.BlockSpec((1, s, bw), lambda bi, h, i: (bi, 0, h))],
        out_specs=pl.BlockSpec((1, tq, bw), lambda bi, h, i: (bi, i, h)),
        out_shape=jax.ShapeDtypeStruct((b, s, width), BF16),
        compiler_params=_cparams(("parallel", "parallel", "arbitrary"), 48),
        name="sb_prompt",
    )(bias.astype(F32), q3, k3, v3)
    return out.reshape(b * s, width)


def _sb_sample_kernel(pt_ref, bias_ref, q_ref, knew_ref, vnew_ref, *refs, n_pages, n_new):
    del pt_ref
    k_pages = refs[:n_pages]
    v_pages = refs[n_pages:2 * n_pages]
    o_ref = refs[2 * n_pages]
    rows = q_ref.shape[1]
    nh = o_ref.shape[2]
    q = q_ref[0].astype(BF16)
    bias = bias_ref[...]
    minus_later = _minus_later_twice(HEAD)

    def own_head(width):
        r = lax.broadcasted_iota(jnp.int32, (rows, width), 0)
        c = lax.broadcasted_iota(jnp.int32, (rows, width), 1)
        return r, c, (r & (nh - 1)) == (c & (nh - 1))

    r, c, own = own_head(knew_ref.shape[1])
    shift = int(math.log2(nh))
    new_mask = own & ((c >> shift) < (r >> shift)) & ((c >> shift) < n_new)
    page_mask = own_head(k_pages[0].shape[2] * math.gcd(n_pages, 4))[2]
    segs = [(lambda: knew_ref[0], lambda: vnew_ref[0], new_mask)]
    pps = math.gcd(n_pages, 4)
    for p0 in reversed(range(0, n_pages, pps)):
        take = lambda refs, p0=p0: jnp.concatenate(
            [refs[p][0, 0].astype(BF16) for p in range(p0, p0 + pps)], axis=0)
        segs.append((lambda take=take: take(k_pages), lambda take=take: take(v_pages), page_mask))
    n = len(segs)
    zs, sps, leaves, sums = [None] * n, [None] * n, [None] * n, [None] * n
    csum = jnp.zeros((rows, 1), F32)
    acc = jnp.zeros((rows, HEAD), F32)
    for step in range(n + 2):
        if step < n:
            zs[step] = _dot_nt(q, segs[step][0]().astype(BF16)) + bias
        i = step - 1
        if 0 <= i < n:
            sps[i] = _softplus_fast(zs[i])
            leaves[i] = jnp.where(segs[i][2], sps[i], 0.0)
            sums[i] = [_dot_exact_rhs(leaves[i][:, blk * HEAD:(blk + 1) * HEAD], minus_later)
                       for blk in range(zs[i].shape[1] // HEAD)]
        i = step - 2
        if 0 <= i < n:
            pieces = [None] * len(sums[i])
            for blk in reversed(range(len(sums[i]))):
                pieces[blk] = sums[i][blk] + csum
                csum = csum - jnp.sum(leaves[i][:, blk * HEAD:(blk + 1) * HEAD], axis=-1, keepdims=True)
            survive = pieces[0] if len(pieces) == 1 else jnp.concatenate(pieces, axis=1)
            w = jnp.where(segs[i][2], jnp.exp(zs[i] - sps[i] + survive), 0.0)
            acc = acc + _dot(w.astype(BF16), segs[i][1]().astype(BF16))
            zs[i] = sps[i] = leaves[i] = sums[i] = None
    o_ref[0] = acc.reshape(rows // nh, nh, HEAD)


def _sb_sample(q, k_new, v_new, bias, cache_k, cache_v, layer, page_table, b, s):
    width = q.shape[1]
    nh = width // HEAD
    n_pages = page_table.shape[1]
    rows = s * nh
    new_rows = HEAD
    assert rows <= new_rows and cache_k.shape[2] == PAGE and nh & (nh - 1) == 0

    key_rows = lambda a: jnp.pad(a.reshape(b, rows, HEAD), ((0, 0), (0, new_rows - rows), (0, 0)))
    flat = lambda c: c.reshape(c.shape[0], c.shape[1], PAGE * nh, HEAD)
    bias_rows = jnp.tile(bias.astype(F32), s).reshape(rows, 1)

    def page_spec(p):
        return pl.BlockSpec((1, 1, PAGE * nh, HEAD), lambda bi, pt: (layer, pt[bi, p], 0, 0))

    grid_spec = pltpu.PrefetchScalarGridSpec(
        num_scalar_prefetch=1,
        grid=(b,),
        in_specs=[pl.BlockSpec((rows, 1), lambda bi, pt: (0, 0)),
                  pl.BlockSpec((1, rows, HEAD), lambda bi, pt: (bi, 0, 0)),
                  pl.BlockSpec((1, new_rows, HEAD), lambda bi, pt: (bi, 0, 0)),
                  pl.BlockSpec((1, new_rows, HEAD), lambda bi, pt: (bi, 0, 0))]
        + [page_spec(p) for p in range(n_pages)] * 2,
        out_specs=pl.BlockSpec((1, s, nh, HEAD), lambda bi, pt: (bi, 0, 0, 0)),
    )
    out = pl.pallas_call(
        functools.partial(_sb_sample_kernel, n_pages=n_pages, n_new=s),
        grid_spec=grid_spec,
        out_shape=jax.ShapeDtypeStruct((b, s, nh, HEAD), F32),
        compiler_params=_cparams(("arbitrary",), 56),
        name="sb_sample",
    )(page_table, bias_rows, q.reshape(b, rows, HEAD), key_rows(k_new), key_rows(v_new),
      *([flat(cache_k)] * n_pages), *([flat(cache_v)] * n_pages))
    return out.reshape(b * s, width)


def _gdn_kernel(raw_ref, ab_ref, gz_ref, convw_ref, conv0_ref, s0_ref, alog_ref, dtb_ref, gn_ref,
                o_ref, s_ref, pad_ref, *, c, n_valid):
    n = pl.program_id(1)
    bb, nh = s_ref.shape[0], s_ref.shape[1]
    qkw = nh * HEAD

    @pl.when(n == 0)
    def _():
        for bi in range(bb):
            pad_ref[bi, 0:8, :] = jnp.zeros((8, pad_ref.shape[2]), F32)
            pad_ref[bi, 5:8, :] = conv0_ref[bi]
        s_ref[...] = s0_ref[...]

    cw = convw_ref[...]
    rc = lax.broadcasted_iota(jnp.int32, (c, c), 0)
    cc = lax.broadcasted_iota(jnp.int32, (c, c), 1)
    upto_c = (cc <= rc).astype(BF16)
    gn = gn_ref[...]

    hpg = min(nh, max(1, GDN_CHUNK // c))
    rows = hpg * c

    def stack(pieces):
        return pieces[0] if len(pieces) == 1 else jnp.concatenate(pieces, axis=0)

    def unit(x):
        return x * lax.rsqrt(jnp.sum(x * x, axis=-1, keepdims=True) + EPS)

    groups, ops = [], []
    for bi in range(bb):
        pad_ref[bi, 8:8 + c, :] = raw_ref[bi]
        conv = pad_ref[bi, 5:5 + c, :] * cw[0:1, :]
        for j in range(1, GDN_CONV):
            conv = conv + pad_ref[bi, 5 + j:5 + j + c, :] * cw[j:j + 1, :]
        pad_ref[bi, 5:8, :] = pad_ref[bi, 5 + c:8 + c, :]
        xs = _silu(conv)

        ab = ab_ref[bi]
        live = lax.broadcasted_iota(jnp.int32, ab.shape, 0) < n_valid
        g_all = jnp.where(live, -jnp.exp(alog_ref[...]) * _softplus(ab + dtb_ref[...]), 0.0)
        beta_all = jnp.where(live, jax.nn.sigmoid(ab), 0.0)
        cum_all = _dot_exact_lhs(upto_c, g_all)
        cum_last = cum_all[c - 1:c, :]
        ecum_all = jnp.exp(cum_all)
        eend_all = jnp.exp(cum_last - cum_all)
        glast_all = jnp.exp(cum_last)
        for g0 in range(0, nh, hpg):
            hs = list(range(g0, g0 + hpg))
            q = unit(stack([xs[:, h * HEAD:(h + 1) * HEAD] for h in hs])) * (HEAD ** -0.5)
            k = unit(stack([xs[:, qkw + h * HEAD:qkw + (h + 1) * HEAD] for h in hs]))
            v = stack([xs[:, 2 * qkw + h * HEAD:2 * qkw + (h + 1) * HEAD] for h in hs])
            col = lambda a, off: stack([a[:, off + h:off + h + 1] for h in hs])
            groups.append((bi, hs))
            ops.append(dict(q=q, k=k, v=v, g=col(g_all, 0), beta=col(beta_all, nh),
                            ecum=col(ecum_all, 0), eend=col(eend_all, 0), glast=glast_all))

    ri = lax.broadcasted_iota(jnp.int32, (rows, rows), 0)
    ci = lax.broadcasted_iota(jnp.int32, (rows, rows), 1)
    shift = int(math.log2(c))
    same = (ri >> shift) == (ci >> shift)
    strict = same & (ri > ci)
    incl = same & (ri >= ci)
    upto = incl.astype(BF16)
    eye = (ri == ci).astype(F32)
    diffs = [_dot_exact_lhs(upto, jnp.where(strict, jnp.broadcast_to(o["g"], (rows, rows)), 0.0), terms=2)
             for o in ops]
    kks = [_dot_bf(o["k"], o["k"], _dot_nt) for o in ops]
    qks = [_dot_bf(o["q"], o["k"], _dot_nt) for o in ops]
    decays = [jnp.where(incl, jnp.exp(d), 0.0) for d in diffs]
    npows = [jnp.where(strict, o["beta"] * kk * dec, 0.0) for o, kk, dec in zip(ops, kks, decays)]
    invs = [eye - n for n in npows]
    for _ in range(shift - 1):
        npows = [_dot_hp(n, n) for n in npows]
        invs = [inv + _dot_hp(inv, n) for inv, n in zip(invs, npows)]
    sols = [_dot_bf(inv, jnp.concatenate([o["v"] * o["beta"], o["k"] * (o["beta"] * o["ecum"])], axis=-1))
            for inv, o in zip(invs, ops)]
    qdecs = [o["q"] * o["ecum"] for o in ops]
    from_state = []
    for (bi, hs), sol, qdec in zip(groups, sols, qdecs):
        for j, h in enumerate(hs):
            r = slice(j * c, (j + 1) * c)
            from_state.append(_dot_bf(jnp.concatenate([sol[r, HEAD:], qdec[r, :]], axis=0), s_ref[bi, h]))
    v_news, outs = [], []
    for gi, (sol, qk, dec) in enumerate(zip(sols, qks, decays)):
        mine = from_state[gi * hpg:(gi + 1) * hpg]
        v_new = sol[:, :HEAD] - stack([m[:c, :] for m in mine])
        v_news.append(v_new)
        outs.append(stack([m[c:, :] for m in mine]) + _dot_bf(qk * dec, v_new))
    for (bi, hs), o, v_new in zip(groups, ops, v_news):
        k_end = o["k"] * o["eend"]
        for j, h in enumerate(hs):
            r = slice(j * c, (j + 1) * c)
            s_ref[bi, h] = (o["glast"][:, h:h + 1] * s_ref[bi, h]
                            + _dot_bf(k_end[r, :], v_new[r, :], _dot_tn))
    for (bi, hs), out in zip(groups, outs):
        normed = _rms(out, gn)
        for j, h in enumerate(hs):
            sl = slice(h * HEAD, (h + 1) * HEAD)
            o_ref[bi, :, sl] = (normed[j * c:(j + 1) * c, :] * _silu(gz_ref[bi, :, sl])).astype(o_ref.dtype)


def _gdn(raw, ab, gz, conv_w, conv0, s0, a_log, dt_bias, out_gain, b, s, c, n_valid):
    nh = s0.shape[1]
    ch = raw.shape[2]
    bb = math.gcd(b, 4) if (s == c and c * nh <= GDN_CHUNK) else 1
    pad_lane = lambda a: jnp.pad(a.astype(F32), (0, HEAD - a.shape[0])).reshape(1, HEAD)
    out, state = pl.pallas_call(
        functools.partial(_gdn_kernel, c=c, n_valid=n_valid),
        grid=(b // bb, s // c),
        in_specs=[pl.BlockSpec((bb, c, ch), lambda bi, n: (bi, n, 0)),
                  pl.BlockSpec((bb, c, HEAD), lambda bi, n: (bi, n, 0)),
                  pl.BlockSpec((bb, c, nh * HEAD), lambda bi, n: (bi, n, 0)),
                  pl.BlockSpec((GDN_CONV, ch), lambda bi, n: (0, 0)),
                  pl.BlockSpec((bb, GDN_CONV - 1, ch), lambda bi, n: (bi, 0, 0)),
                  pl.BlockSpec((bb, nh, HEAD, HEAD), lambda bi, n: (bi, 0, 0, 0)),
                  pl.BlockSpec((1, HEAD), lambda bi, n: (0, 0)),
                  pl.BlockSpec((1, HEAD), lambda bi, n: (0, 0)),
                  pl.BlockSpec((1, HEAD), lambda bi, n: (0, 0))],
        out_specs=[pl.BlockSpec((bb, c, nh * HEAD), lambda bi, n: (bi, n, 0)),
                   pl.BlockSpec((bb, nh, HEAD, HEAD), lambda bi, n: (bi, 0, 0, 0))],
        out_shape=[jax.ShapeDtypeStruct((b, s, nh * HEAD), BF16),
                   jax.ShapeDtypeStruct(s0.shape, F32)],
        scratch_shapes=[pltpu.VMEM((bb, 8 + c, ch), F32)],
        compiler_params=_cparams(("parallel", "arbitrary"), 32),
        name="gdn",
    )(raw, ab, gz, conv_w, conv0, s0, pad_lane(a_log), pad_lane(dt_bias), out_gain.reshape(1, HEAD))
    return out, state


def _mix_out_kernel(*refs, n_a):
    a_refs = refs[:n_a]
    w_ref, x_ref, g_ref, wq_ref, qg_ref, x_out, q_out = refs[n_a:]
    acc = x_ref[...]
    off = 0
    for a in a_refs:
        ka = a.shape[1]
        acc = acc + _dot(a[...].astype(BF16), w_ref[off:off + ka, :])
        off += ka
    x_out[...] = acc
    xq = _dot(_rms(acc, g_ref[...]).astype(BF16), wq_ref[...])
    qg = qg_ref[...]
    for s in range(xq.shape[1] // HEAD):
        sl = slice(s * HEAD, (s + 1) * HEAD)
        q_out[:, sl] = _rms(xq[:, sl], qg).astype(q_out.dtype)


def _mix_out(acts, w_bf16, x, next_gain, wq_bf16, q_head_gain):
    t, d = x.shape
    qw = wq_bf16.shape[1]
    tm = min(t, 256)
    return pl.pallas_call(
        functools.partial(_mix_out_kernel, n_a=len(acts)),
        grid=(t // tm,),
        in_specs=[pl.BlockSpec((tm, a.shape[1]), lambda i: (i, 0)) for a in acts]
        + [pl.BlockSpec(w_bf16.shape, lambda i: (0, 0)),
           pl.BlockSpec((tm, d), lambda i: (i, 0)),
           pl.BlockSpec((1, d), lambda i: (0, 0)),
           pl.BlockSpec(wq_bf16.shape, lambda i: (0, 0)),
           pl.BlockSpec((1, HEAD), lambda i: (0, 0))],
        out_specs=[pl.BlockSpec((tm, d), lambda i: (i, 0)), pl.BlockSpec((tm, qw), lambda i: (i, 0))],
        out_shape=[jax.ShapeDtypeStruct((t, d), F32), jax.ShapeDtypeStruct((t, qw), BF16)],
        compiler_params=_cparams(("parallel",), 48),
        name="mix_out",
    )(*acts, w_bf16, x, next_gain.reshape(1, d), wq_bf16, q_head_gain.reshape(1, HEAD))


def _xattn_prompt_kernel(q_ref, k_ref, v_ref, o_ref, *, scale):
    for h in range(q_ref.shape[2] // HEAD):
        sl = slice(h * HEAD, (h + 1) * HEAD)
        logits = _dot_nt(q_ref[0, :, sl], k_ref[0, :, sl].astype(BF16)) * scale
        p = jnp.exp(logits - jnp.max(logits, axis=-1, keepdims=True))
        attn = p / jnp.sum(p, axis=-1, keepdims=True)
        o_ref[0, :, sl] = _dot(attn.astype(BF16), v_ref[0, :, sl].astype(BF16)).astype(o_ref.dtype)


def _xattn_prompt(xq, mem_k, mem_v, b, s):
    width = xq.shape[1]
    m = mem_k.shape[1]
    tq = min(s, 512)
    out = pl.pallas_call(
        functools.partial(_xattn_prompt_kernel, scale=HEAD ** -0.5),
        grid=(b, s // tq),
        in_specs=[pl.BlockSpec((1, tq, width), lambda bi, i: (bi, i, 0)),
                  pl.BlockSpec((1, m, width), lambda bi, i: (bi, 0, 0)),
                  pl.BlockSpec((1, m, width), lambda bi, i: (bi, 0, 0))],
        out_specs=pl.BlockSpec((1, tq, width), lambda bi, i: (bi, i, 0)),
        out_shape=jax.ShapeDtypeStruct((b, s, width), BF16),
        compiler_params=_cparams(("parallel", "arbitrary"), 32),
        name="xattn_prompt",
    )(xq.reshape(b, s, width), mem_k, mem_v)
    return out.reshape(b * s, width)


def _xattn_sample_kernel(q_ref, k_ref, v_ref, o_ref, *, n_new, scale):
    bb, nh = q_ref.shape[0], q_ref.shape[1]
    m = k_ref.shape[2] // nh
    pairs = [(i, h) for i in range(bb) for h in range(nh)]
    logits = [_dot_nt(q_ref[i, h].astype(BF16), k_ref[0, i, pl.ds(h, m, stride=nh), :].astype(BF16)) * scale
              for i, h in pairs]
    ps = [jnp.exp(lg - jnp.max(lg, axis=-1, keepdims=True)) for lg in logits]
    attns = [(p / jnp.sum(p, axis=-1, keepdims=True)).astype(BF16) for p in ps]
    for (i, h), attn in zip(pairs, attns):
        o_h = _dot(attn, v_ref[0, i, pl.ds(h, m, stride=nh), :].astype(BF16))
        o_ref[i, :, h * HEAD:(h + 1) * HEAD] = o_h[0:n_new, :]


def _xattn_sample(xq, mem_k, mem_v, layer, b, s):
    width = xq.shape[1]
    nh = width // HEAD
    m = mem_k.shape[2]
    qrows = 8
    bb = math.gcd(b, 4)
    q4 = jnp.pad(xq.reshape(b, s, nh, HEAD).transpose(0, 2, 1, 3), ((0, 0), (0, 0), (0, qrows - s), (0, 0)))
    mem_spec = pl.BlockSpec((1, bb, m * nh, HEAD), lambda bi: (layer, bi, 0, 0))
    flat = lambda c: c.reshape(c.shape[0], c.shape[1], m * nh, HEAD)
    out = pl.pallas_call(
        functools.partial(_xattn_sample_kernel, n_new=s, scale=HEAD ** -0.5),
        grid=(b // bb,),
        in_specs=[pl.BlockSpec((bb, nh, qrows, HEAD), lambda bi: (bi, 0, 0, 0)), mem_spec, mem_spec],
        out_specs=pl.BlockSpec((bb, s, width), lambda bi: (bi, 0, 0)),
        out_shape=jax.ShapeDtypeStruct((b, s, width), F32),
        compiler_params=_cparams(("parallel",), 32),
        name="xattn_sample",
    )(q4, flat(mem_k), flat(mem_v))
    return out.reshape(b * s, width)


def _ffn_kernel(x_ref, a_ref, wa_ref, g_ref, wg_ref, wu_ref, wd_ref, o_ref, h_ref):
    j = pl.program_id(1)

    @pl.when(j == 0)
    def _():
        x = x_ref[...] + _dot(a_ref[...].astype(BF16), wa_ref[...])
        h_ref[...] = _rms(x, g_ref[...]).astype(h_ref.dtype)
        o_ref[...] = x

    h = h_ref[...]
    gate = _dot(h, wg_ref[...].astype(BF16))
    up = _dot(h, wu_ref[...].astype(BF16))
    act = (_silu(gate) * up).astype(BF16)
    o_ref[...] += _dot(act, wd_ref[...].astype(BF16))


def _ffn(x, attn, w_attn_bf16, gain, w_gate_up, w_down):
    t, d = x.shape
    f = w_down.shape[0]
    tm = min(t, 1024)
    tf = 256
    assert f % tf == 0 and t % tm == 0
    nf = f // tf
    return pl.pallas_call(
        _ffn_kernel,
        grid=(t // tm, nf),
        in_specs=[pl.BlockSpec((tm, d), lambda i, j: (i, 0)),
                  pl.BlockSpec((tm, attn.shape[1]), lambda i, j: (i, 0)),
                  pl.BlockSpec(w_attn_bf16.shape, lambda i, j: (0, 0)),
                  pl.BlockSpec((1, d), lambda i, j: (0, 0)),
                  pl.BlockSpec((d, tf), lambda i, j: (0, j)),
                  pl.BlockSpec((d, tf), lambda i, j: (0, nf + j)),
                  pl.BlockSpec((tf, d), lambda i, j: (j, 0))],
        out_specs=pl.BlockSpec((tm, d), lambda i, j: (i, 0)),
        out_shape=jax.ShapeDtypeStruct((t, d), F32),
        scratch_shapes=[pltpu.VMEM((tm, d), BF16)],
        compiler_params=_cparams(("parallel", "arbitrary"), 60),
        name="ffn",
    )(x, attn, w_attn_bf16, gain.reshape(1, d), w_gate_up, w_gate_up, w_down)


def _decoder_layer(x, b, s, p, sb_attend, xattend, conv0, s0):
    d = x.shape[1]
    sbw = SB_HEADS * HEAD
    gw = GDN_HEADS * HEAD
    h = _rmsnorm_bf16(x, p["norm_mix_g"])
    q, k_new, v_new, raw, gz, ab = _in_proj(
        h, p["w_in_t"], p["sb_q_norm_g"] * (HEAD ** -0.5), p["sb_k_norm_g"], p["sb_q_dtype"],
        (sbw, sbw, sbw, 3 * gw), gw, 2 * GDN_HEADS)
    o_sb = sb_attend(q, k_new, v_new)

    c = math.gcd(s, GDN_CHUNK)
    raw3, ab3, gz3 = raw.reshape(b, s, 3 * gw), ab.reshape(b, s, ab.shape[1]), gz.reshape(b, s, gw)
    conv_new = raw3[:, s - (GDN_CONV - 1):, :]
    n_valid = c
    if c < 8:
        padn = ((0, 0), (0, 8 - s), (0, 0))
        raw3, ab3, gz3 = jnp.pad(raw3, padn), jnp.pad(ab3, padn), jnp.pad(gz3, padn)
        c = 8
    o_gdn, state = _gdn(raw3, ab3, gz3, p["gdn_conv_w"], conv0, s0, p["gdn_a_log"], p["gdn_dt_bias"],
                        p["gdn_out_norm_g"], b, raw3.shape[1], c, n_valid)
    o_gdn = o_gdn[:, :s, :].reshape(b * s, gw)

    x1, xq = _mix_out([o_sb, o_gdn], p["w_out_bf"], x, p["norm_x_g"], p["w_xq_bf"], p["x_q_norm_g"])
    xo = xattend(xq)
    y = _ffn(x1, xo, p["w_xo_bf"], p["norm_ffn_g"], p["w_gate_up"], p["w_down"])
    return y, k_new, v_new, state, conv_new


def kernel(x_prompt, x_sample, mem_prompt, cache_sb_k, cache_sb_v, page_table, state_gdn, state_gdn_conv, cache_mem_k, cache_mem_v, norm_mix_g, w_in, sb_q_norm_g, sb_k_norm_g, sb_logit_bias, gdn_conv_w, gdn_a_log, gdn_dt_bias, gdn_out_norm_g, w_out, norm_x_g, norm_mem_g, w_xq, w_mk, w_mv, x_q_norm_g, x_k_norm_g, w_xo, norm_ffn_g, w_gate_up, w_down):
    bp, sp_, d = x_prompt.shape
    bs, ss, _ = x_sample.shape
    depth = w_in.shape[0]
    m = mem_prompt.shape[1]
    sbw = SB_HEADS * HEAD
    gw = GDN_HEADS * HEAD
    xw = X_HEADS * HEAD
    ch = 3 * gw
    ab0 = 3 * sbw + ch
    yp = x_prompt.reshape(bp * sp_, d)
    ys = x_sample.reshape(bs * ss, d)
    outs = [[] for _ in range(10)]
    for l in range(depth):
        p = {
            "norm_mix_g": norm_mix_g[l], "w_in_t": jnp.swapaxes(w_in[l], 0, 1), "sb_q_norm_g": sb_q_norm_g[l],
            "sb_k_norm_g": sb_k_norm_g[l],
            "gdn_conv_w": gdn_conv_w[l], "gdn_a_log": gdn_a_log[l], "gdn_dt_bias": gdn_dt_bias[l],
            "gdn_out_norm_g": gdn_out_norm_g[l], "norm_x_g": norm_x_g[l], "x_q_norm_g": x_q_norm_g[l],
            "norm_ffn_g": norm_ffn_g[l], "w_gate_up": w_gate_up[l], "w_down": w_down[l],
            "w_out_bf": w_out[l].astype(BF16), "w_xq_bf": w_xq[l].astype(BF16), "w_xo_bf": w_xo[l].astype(BF16),
        }
        bias = sb_logit_bias[l]
        hm = _rmsnorm_bf16(mem_prompt.reshape(bp * m, d), norm_mem_g[l])
        mem_k = _proj(hm, w_mk[l], 0, xw, F32, x_k_norm_g[l]).reshape(bp, m, xw)
        mem_v = _proj(hm, w_mv[l], 0, xw, F32).reshape(bp, m, xw)

        pp = dict(p, sb_q_dtype=BF16)
        yp, kp, vp, stp, cvp = _decoder_layer(
            yp, bp, sp_, pp,
            lambda q, k, v: _sb_prompt(q, k, v, bias, bp, sp_),
            lambda xq: _xattn_prompt(xq, mem_k, mem_v, bp, sp_),
            jnp.zeros((bp, GDN_CONV - 1, ch), F32), jnp.zeros((bp, GDN_HEADS, HEAD, HEAD), F32))

        ps = dict(p, sb_q_dtype=F32)
        ys, ks, vs, sts, cvs = _decoder_layer(
            ys, bs, ss, ps,
            lambda q, k, v: _sb_sample(q, k, v, bias, cache_sb_k, cache_sb_v, l, page_table, bs, ss),
            lambda xq: _xattn_sample(xq, cache_mem_k, cache_mem_v, l, bs, ss),
            state_gdn_conv[l], state_gdn[l])

        for acc, val in zip(outs, (
                kp.reshape(bp, sp_, SB_HEADS, HEAD), vp.reshape(bp, sp_, SB_HEADS, HEAD),
                ks.reshape(bs, ss, SB_HEADS, HEAD), vs.reshape(bs, ss, SB_HEADS, HEAD),
                stp.astype(state_gdn.dtype), cvp, sts.astype(state_gdn.dtype), cvs,
                mem_k.reshape(bp, m, X_HEADS, HEAD), mem_v.reshape(bp, m, X_HEADS, HEAD))):
            acc.append(val)
    return (yp.reshape(bp, sp_, d), ys.reshape(bs, ss, d)) + tuple(jnp.stack(o) for o in outs)
```

```python
import functools
import math

import jax
import jax.numpy as jnp
from jax import lax
from jax.experimental import pallas as pl
from jax.experimental.pallas import tpu as pltpu

F32 = jnp.float32
BF16 = jnp.bfloat16
EPS = 1e-6
HEAD = 128
SB_HEADS = 8
GDN_HEADS = 8
X_HEADS = 4
PAGE = 128
GDN_CONV = 4
GDN_CHUNK = 64
MIB = 1 << 20


def _cparams(sem, vmem_mib):
    return pltpu.CompilerParams(dimension_semantics=sem, vmem_limit_bytes=vmem_mib * MIB)


def _dot(a, b):
    return jnp.dot(a, b, preferred_element_type=F32)


def _dot_nt(a, b):
    return lax.dot_general(a, b, (((1,), (1,)), ((), ())), preferred_element_type=F32)


def _dot_tn(a, b):
    return lax.dot_general(a, b, (((0,), (0,)), ((), ())), preferred_element_type=F32)


def _split2(a):
    hi = a.astype(BF16)
    lo = (a - hi.astype(F32)).astype(BF16)
    return hi, lo


def _dot_hp(a, b, dot=_dot):
    ah, al = _split2(a)
    bh, bl = _split2(b)
    return dot(ah, bh) + (dot(ah, bl) + dot(al, bh))


def _dot_bf(a, b, dot=_dot):
    return dot(a.astype(BF16), b.astype(BF16))


def _dot_exact_lhs(m01, b, terms=3):
    acc = None
    for _ in range(terms):
        piece = b.astype(BF16)
        b = b - piece.astype(F32)
        part = _dot(m01, piece)
        acc = part if acc is None else acc + part
    return acc


def _dot_exact_rhs(a, m_twice):
    hi, lo = _split2(a)
    return _dot(jnp.concatenate([hi, lo], axis=1), m_twice)


def _softplus(z):
    return jnp.maximum(z, 0.0) + jnp.log1p(jnp.exp(-jnp.abs(z)))


def _softplus_fast(z):
    return jnp.maximum(z, 0.0) + jnp.log(1.0 + jnp.exp(-jnp.abs(z)))


def _silu(x):
    return x * jax.nn.sigmoid(x)


def _rms(y, g):
    ms = jnp.mean(y * y, axis=-1, keepdims=True)
    return y * lax.rsqrt(ms + EPS) * g


def _rmsnorm_kernel(x_ref, g_ref, o_ref):
    o_ref[...] = _rms(x_ref[...], g_ref[...]).astype(o_ref.dtype)


def _rmsnorm_bf16(x, g):
    t, d = x.shape
    tm = min(t, 512)
    return pl.pallas_call(
        _rmsnorm_kernel,
        grid=(t // tm,),
        in_specs=[pl.BlockSpec((tm, d), lambda i: (i, 0)), pl.BlockSpec((1, d), lambda i: (0, 0))],
        out_specs=pl.BlockSpec((tm, d), lambda i: (i, 0)),
        out_shape=jax.ShapeDtypeStruct((t, d), BF16),
        compiler_params=_cparams(("parallel",), 32),
        name="rmsnorm",
    )(x, g.reshape(1, d))


def _proj_kernel(x_ref, w_ref, g_ref, o_ref, *, head_norm):
    acc = _dot(x_ref[...], w_ref[...].astype(BF16))
    if head_norm:
        g = g_ref[...]
        for s in range(acc.shape[1] // HEAD):
            sl = slice(s * HEAD, (s + 1) * HEAD)
            o_ref[:, sl] = _rms(acc[:, sl], g).astype(o_ref.dtype)
    else:
        o_ref[...] = acc.astype(o_ref.dtype)


def _proj(h, w, col0, n_cols, out_dtype, head_gain=None):
    t, k = h.shape
    tm = min(t, 2048)
    tn = min(512, n_cols)
    assert col0 % tn == 0 and n_cols % tn == 0 and t % tm == 0
    cb = col0 // tn
    g = jnp.ones((1, HEAD), F32) if head_gain is None else head_gain.reshape(1, HEAD).astype(F32)
    return pl.pallas_call(
        functools.partial(_proj_kernel, head_norm=head_gain is not None),
        grid=(t // tm, n_cols // tn),
        in_specs=[pl.BlockSpec((tm, k), lambda i, j: (i, 0)),
                  pl.BlockSpec((k, tn), lambda i, j: (0, cb + j)),
                  pl.BlockSpec((1, HEAD), lambda i, j: (0, 0))],
        out_specs=pl.BlockSpec((tm, tn), lambda i, j: (i, j)),
        out_shape=jax.ShapeDtypeStruct((t, n_cols), out_dtype),
        compiler_params=_cparams(("parallel", "arbitrary"), 48),
        name="proj",
    )(h, w, g)


def _in_proj_kernel(h_ref, wm_ref, wz_ref, wl_ref, qg_ref, kg_ref,
                    q_ref, k_ref, v_ref, raw_ref, gz_ref, ab_ref, *, edges):
    j = pl.program_id(1)

    def head_normed(acc, g_ref):
        g = g_ref[...]
        return jnp.concatenate(
            [_rms(acc[:, s * HEAD:(s + 1) * HEAD], g) for s in range(acc.shape[1] // HEAD)], axis=1)

    def group(w_ref, out_ref, g_ref=None):
        def run():
            acc = _dot_nt(h_ref[...], w_ref[...].astype(BF16))
            out_ref[...] = (acc if g_ref is None else head_normed(acc, g_ref)).astype(out_ref.dtype)
        return run

    runs = (group(wm_ref, q_ref, qg_ref), group(wm_ref, k_ref, kg_ref), group(wm_ref, v_ref),
            group(wm_ref, raw_ref), group(wz_ref, gz_ref), group(wl_ref, ab_ref))
    for lo, hi, run in zip((0,) + edges[:-1], edges, runs):
        pl.when((j >= lo) & (j < hi))(run)


def _in_proj(h, w_t, q_gain, k_gain, q_dtype, main_widths, gate_width, n_logit):
    t, k = h.shape
    tm = min(t, 2048)
    tn = 256 if t > 1024 else 512
    n_main = sum(main_widths)
    widths = tuple(main_widths) + (gate_width, tn)
    assert all(w % tn == 0 for w in widths) and t % tm == 0 and n_logit <= tn
    nblk = [w // tn for w in widths]
    edges = tuple(int(sum(nblk[:i + 1])) for i in range(len(nblk)))
    starts = (0,) + edges[:-1]
    w_gate = w_t[n_main + n_logit:n_main + n_logit + gate_width]
    w_logit = jnp.pad(w_t[n_main:n_main + n_logit], ((0, tn - n_logit), (0, 0)))

    def swept(lo, n):
        return lambda i, j: (i, jnp.clip(j - lo, 0, n - 1))

    n_main_blk, gate_lo, n_gate_blk = edges[len(main_widths) - 1], starts[-2], nblk[-2]
    dtypes = (q_dtype,) + (F32,) * (len(widths) - 1)
    gain = lambda g: g.reshape(1, HEAD).astype(F32)
    return pl.pallas_call(
        functools.partial(_in_proj_kernel, edges=edges),
        grid=(t // tm, edges[-1]),
        in_specs=[pl.BlockSpec((tm, k), lambda i, j: (i, 0)),
                  pl.BlockSpec((tn, k), lambda i, j: (jnp.minimum(j, n_main_blk - 1), 0)),
                  pl.BlockSpec((tn, k), lambda i, j: (jnp.clip(j - gate_lo, 0, n_gate_blk - 1), 0)),
                  pl.BlockSpec((tn, k), lambda i, j: (0, 0), pipeline_mode=pl.Buffered(1)),
                  pl.BlockSpec((1, HEAD), lambda i, j: (0, 0)),
                  pl.BlockSpec((1, HEAD), lambda i, j: (0, 0))],
        out_specs=[pl.BlockSpec((tm, tn), swept(lo, n)) for lo, n in zip(starts, nblk)],
        out_shape=[jax.ShapeDtypeStruct((t, w), dt) for w, dt in zip(widths, dtypes)],
        compiler_params=_cparams(("parallel", "arbitrary"), 56),
        name="in_proj",
    )(h, w_t, w_gate, w_logit, gain(q_gain), gain(k_gain))


def _minus_later_twice(n):
    row = lax.broadcasted_iota(jnp.int32, (n, n), 0)
    col = lax.broadcasted_iota(jnp.int32, (n, n), 1)
    m = jnp.where(row > col, -1.0, 0.0).astype(BF16)
    return jnp.concatenate([m, m], axis=0)


def _sb_prompt_kernel(bias_ref, q_ref, k_ref, v_ref, o_ref, *, tq, hpb):
    hb = pl.program_id(1)
    i = pl.program_id(2)
    row = lax.broadcasted_iota(jnp.int32, (tq, tq), 0)
    col = lax.broadcasted_iota(jnp.int32, (tq, tq), 1)
    minus_later = _minus_later_twice(tq)
    causal = col < row
    heads = range(hpb)
    qs = [q_ref[0, :, g * HEAD:(g + 1) * HEAD] for g in heads]
    biases = [bias_ref[hb * hpb + g] for g in heads]

    def block(j, carry, masked):
        start = pl.multiple_of(j * tq, tq)
        kb = k_ref[0, pl.ds(start, tq), :].astype(BF16)
        vb = v_ref[0, pl.ds(start, tq), :].astype(BF16)
        zs = [_dot_nt(qs[g], kb[:, g * HEAD:(g + 1) * HEAD]) + biases[g] for g in heads]
        sps = [_softplus_fast(z) for z in zs]
        leaves = [jnp.where(causal, sp, 0.0) if masked else sp for sp in sps]
        survs = [_dot_exact_rhs(lv, minus_later) for lv in leaves]
        out = []
        for g in heads:
            csum, acc = carry[g]
            w = jnp.exp(zs[g] - sps[g] + survs[g] + csum)
            if masked:
                w = jnp.where(causal, w, 0.0)
            acc = acc + _dot(w.astype(BF16), vb[:, g * HEAD:(g + 1) * HEAD])
            out.append((csum - jnp.sum(leaves[g], axis=-1, keepdims=True), acc))
        return tuple(out)

    init = tuple((jnp.zeros((tq, 1), F32), jnp.zeros((tq, HEAD), F32)) for _ in heads)
    carry = block(i, init, True)
    carry = lax.fori_loop(0, i, lambda jj, c: block(i - 1 - jj, c, False), carry)
    for g in heads:
        o_ref[0, :, g * HEAD:(g + 1) * HEAD] = carry[g][1].astype(o_ref.dtype)


def _sb_prompt(q, k, v, bias, b, s):
    width = q.shape[1]
    nh = width // HEAD
    hpb = 8
    tq = min(s, 256)
    assert s % tq == 0 and nh % hpb == 0
    bw = hpb * HEAD
    q3, k3, v3 = (a.reshape(b, s, width) for a in (q, k, v))
    out = pl.pallas_call(
        functools.partial(_sb_prompt_kernel, tq=tq, hpb=hpb),
        grid=(b, nh // hpb, s // tq),
        in_specs=[pl.BlockSpec(memory_space=pltpu.SMEM),
                  pl.BlockSpec((1, tq, bw), lambda bi, h, i: (bi, i, h)),
                  pl.BlockSpec((1, s, bw), lambda bi, h, i: (bi, 0, h)),
                  pl.BlockSpec((1, s, bw), lambda bi, h, i: (bi, 0, h))],
        out_specs=pl.BlockSpec((1, tq, bw), lambda bi, h, i: (bi, i, h)),
        out_shape=jax.ShapeDtypeStruct((b, s, width), BF16),
        compiler_params=_cparams(("parallel", "parallel", "arbitrary"), 48),
        name="sb_prompt",
    )(bias.astype(F32), q3, k3, v3)
    return out.reshape(b * s, width)


def _sb_sample_kernel(pt_ref, bias_ref, q_ref, knew_ref, vnew_ref, *refs, n_pages, n_new):
    del pt_ref
    k_pages = refs[:n_pages]
    v_pages = refs[n_pages:2 * n_pages]
    o_ref = refs[2 * n_pages]
    rows = q_ref.shape[1]
    nh = o_ref.shape[2]
    q = q_ref[0].astype(BF16)
    bias = bias_ref[...]
    minus_later = _minus_later_twice(HEAD)

    def own_head(width):
        r = lax.broadcasted_iota(jnp.int32, (rows, width), 0)
        c = lax.broadcasted_iota(jnp.int32, (rows, width), 1)
        return r, c, (r & (nh - 1)) == (c & (nh - 1))

    r, c, own = own_head(knew_ref.shape[1])
    shift = int(math.log2(nh))
    new_mask = own & ((c >> shift) < (r >> shift)) & ((c >> shift) < n_new)
    page_mask = own_head(k_pages[0].shape[2] * math.gcd(n_pages, 4))[2]
    segs = [(lambda: knew_ref[0], lambda: vnew_ref[0], new_mask)]
    pps = math.gcd(n_pages, 4)
    for p0 in reversed(range(0, n_pages, pps)):
        take = lambda refs, p0=p0: jnp.concatenate(
            [refs[p][0, 0].astype(BF16) for p in range(p0, p0 + pps)], axis=0)
        segs.append((lambda take=take: take(k_pages), lambda take=take: take(v_pages), page_mask))
    n = len(segs)
    zs, sps, leaves, sums = [None] * n, [None] * n, [None] * n, [None] * n
    csum = jnp.zeros((rows, 1), F32)
    acc = jnp.zeros((rows, HEAD), F32)
    for step in range(n + 2):
        if step < n:
            zs[step] = _dot_nt(q, segs[step][0]().astype(BF16)) + bias
        i = step - 1
        if 0 <= i < n:
            sps[i] = _softplus_fast(zs[i])
            leaves[i] = jnp.where(segs[i][2], sps[i], 0.0)
            sums[i] = [_dot_exact_rhs(leaves[i][:, blk * HEAD:(blk + 1) * HEAD], minus_later)
                       for blk in range(zs[i].shape[1] // HEAD)]
        i = step - 2
        if 0 <= i < n:
            pieces = [None] * len(sums[i])
            for blk in reversed(range(len(sums[i]))):
                pieces[blk] = sums[i][blk] + csum
                csum = csum - jnp.sum(leaves[i][:, blk * HEAD:(blk + 1) * HEAD], axis=-1, keepdims=True)
            survive = pieces[0] if len(pieces) == 1 else jnp.concatenate(pieces, axis=1)
            w = jnp.where(segs[i][2], jnp.exp(zs[i] - sps[i] + survive), 0.0)
            acc = acc + _dot(w.astype(BF16), segs[i][1]().astype(BF16))
            zs[i] = sps[i] = leaves[i] = sums[i] = None
    o_ref[0] = acc.reshape(rows // nh, nh, HEAD)


def _sb_sample(q, k_new, v_new, bias, cache_k, cache_v, layer, page_table, b, s):
    width = q.shape[1]
    nh = width // HEAD
    n_pages = page_table.shape[1]
    rows = s * nh
    new_rows = HEAD
    assert rows <= new_rows and cache_k.shape[2] == PAGE and nh & (nh - 1) == 0

    key_rows = lambda a: jnp.pad(a.reshape(b, rows, HEAD), ((0, 0), (0, new_rows - rows), (0, 0)))
    flat = lambda c: c.reshape(c.shape[0], c.shape[1], PAGE * nh, HEAD)
    bias_rows = jnp.tile(bias.astype(F32), s).reshape(rows, 1)

    def page_spec(p):
        return pl.BlockSpec((1, 1, PAGE * nh, HEAD), lambda bi, pt: (layer, pt[bi, p], 0, 0))

    grid_spec = pltpu.PrefetchScalarGridSpec(
        num_scalar_prefetch=1,
        grid=(b,),
        in_specs=[pl.BlockSpec((rows, 1), lambda bi, pt: (0, 0)),
                  pl.BlockSpec((1, rows, HEAD), lambda bi, pt: (bi, 0, 0)),
                  pl.BlockSpec((1, new_rows, HEAD), lambda bi, pt: (bi, 0, 0)),
                  pl.BlockSpec((1, new_rows, HEAD), lambda bi, pt: (bi, 0, 0))]
        + [page_spec(p) for p in range(n_pages)] * 2,
        out_specs=pl.BlockSpec((1, s, nh, HEAD), lambda bi, pt: (bi, 0, 0, 0)),
    )
    out = pl.pallas_call(
        functools.partial(_sb_sample_kernel, n_pages=n_pages, n_new=s),
        grid_spec=grid_spec,
        out_shape=jax.ShapeDtypeStruct((b, s, nh, HEAD), F32),
        compiler_params=_cparams(("arbitrary",), 56),
        name="sb_sample",
    )(page_table, bias_rows, q.reshape(b, rows, HEAD), key_rows(k_new), key_rows(v_new),
      *([flat(cache_k)] * n_pages), *([flat(cache_v)] * n_pages))
    return out.reshape(b * s, width)


def _gdn_kernel(raw_ref, ab_ref, gz_ref, convw_ref, conv0_ref, s0_ref, alog_ref, dtb_ref, gn_ref,
                o_ref, s_ref, pad_ref, *, c, n_valid):
    n = pl.program_id(1)
    bb, nh = s_ref.shape[0], s_ref.shape[1]
    qkw = nh * HEAD

    @pl.when(n == 0)
    def _():
        for bi in range(bb):
            pad_ref[bi, 0:8, :] = jnp.zeros((8, pad_ref.shape[2]), F32)
            pad_ref[bi, 5:8, :] = conv0_ref[bi]
        s_ref[...] = s0_ref[...]

    cw = convw_ref[...]
    rc = lax.broadcasted_iota(jnp.int32, (c, c), 0)
    cc = lax.broadcasted_iota(jnp.int32, (c, c), 1)
    upto_c = (cc <= rc).astype(BF16)
    gn = gn_ref[...]

    hpg = min(nh, max(1, GDN_CHUNK // c))
    rows = hpg * c

    def stack(pieces):
        return pieces[0] if len(pieces) == 1 else jnp.concatenate(pieces, axis=0)

    def unit(x):
        return x * lax.rsqrt(jnp.sum(x * x, axis=-1, keepdims=True) + EPS)

    groups, ops = [], []
    for bi in range(bb):
        pad_ref[bi, 8:8 + c, :] = raw_ref[bi]
        conv = pad_ref[bi, 5:5 + c, :] * cw[0:1, :]
        for j in range(1, GDN_CONV):
            conv = conv + pad_ref[bi, 5 + j:5 + j + c, :] * cw[j:j + 1, :]
        pad_ref[bi, 5:8, :] = pad_ref[bi, 5 + c:8 + c, :]
        xs = _silu(conv)

        ab = ab_ref[bi]
        live = lax.broadcasted_iota(jnp.int32, ab.shape, 0) < n_valid
        g_all = jnp.where(live, -jnp.exp(alog_ref[...]) * _softplus(ab + dtb_ref[...]), 0.0)
        beta_all = jnp.where(live, jax.nn.sigmoid(ab), 0.0)
        cum_all = _dot_exact_lhs(upto_c, g_all)
        cum_last = cum_all[c - 1:c, :]
        ecum_all = jnp.exp(cum_all)
        eend_all = jnp.exp(cum_last - cum_all)
        glast_all = jnp.exp(cum_last)
        for g0 in range(0, nh, hpg):
            hs = list(range(g0, g0 + hpg))
            q = unit(stack([xs[:, h * HEAD:(h + 1) * HEAD] for h in hs])) * (HEAD ** -0.5)
            k = unit(stack([xs[:, qkw + h * HEAD:qkw + (h + 1) * HEAD] for h in hs]))
            v = stack([xs[:, 2 * qkw + h * HEAD:2 * qkw + (h + 1) * HEAD] for h in hs])
            col = lambda a, off: stack([a[:, off + h:off + h + 1] for h in hs])
            groups.append((bi, hs))
            ops.append(dict(q=q, k=k, v=v, g=col(g_all, 0), beta=col(beta_all, nh),
                            ecum=col(ecum_all, 0), eend=col(eend_all, 0), glast=glast_all))

    ri = lax.broadcasted_iota(jnp.int32, (rows, rows), 0)
    ci = lax.broadcasted_iota(jnp.int32, (rows, rows), 1)
    shift = int(math.log2(c))
    same = (ri >> shift) == (ci >> shift)
    strict = same & (ri > ci)
    incl = same & (ri >= ci)
    upto = incl.astype(BF16)
    eye = (ri == ci).astype(F32)
    diffs = [_dot_exact_lhs(upto, jnp.where(strict, jnp.broadcast_to(o["g"], (rows, rows)), 0.0), terms=2)
             for o in ops]
    kks = [_dot_bf(o["k"], o["k"], _dot_nt) for o in ops]
    qks = [_dot_bf(o["q"], o["k"], _dot_nt) for o in ops]
    decays = [jnp.where(incl, jnp.exp(d), 0.0) for d in diffs]
    npows = [jnp.where(strict, o["beta"] * kk * dec, 0.0) for o, kk, dec in zip(ops, kks, decays)]
    invs = [eye - n for n in npows]
    for _ in range(shift - 1):
        npows = [_dot_bf(n, n) for n in npows]
        invs = [inv + _dot_bf(inv, n) for inv, n in zip(invs, npows)]
    sols = [_dot_bf(inv, jnp.concatenate([o["v"] * o["beta"], o["k"] * (o["beta"] * o["ecum"])], axis=-1))
            for inv, o in zip(invs, ops)]
    qdecs = [o["q"] * o["ecum"] for o in ops]
    from_state = []
    for (bi, hs), sol, qdec in zip(groups, sols, qdecs):
        for j, h in enumerate(hs):
            r = slice(j * c, (j + 1) * c)
            from_state.append(_dot_bf(jnp.concatenate([sol[r, HEAD:], qdec[r, :]], axis=0), s_ref[bi, h]))
    v_news, outs = [], []
    for gi, (sol, qk, dec) in enumerate(zip(sols, qks, decays)):
        mine = from_state[gi * hpg:(gi + 1) * hpg]
        v_new = sol[:, :HEAD] - stack([m[:c, :] for m in mine])
        v_news.append(v_new)
        outs.append(stack([m[c:, :] for m in mine]) + _dot_bf(qk * dec, v_new))
    for (bi, hs), o, v_new in zip(groups, ops, v_news):
        k_end = o["k"] * o["eend"]
        for j, h in enumerate(hs):
            r = slice(j * c, (j + 1) * c)
            s_ref[bi, h] = (o["glast"][:, h:h + 1] * s_ref[bi, h]
                            + _dot_bf(k_end[r, :], v_new[r, :], _dot_tn))
    for (bi, hs), out in zip(groups, outs):
        normed = _rms(out, gn)
        for j, h in enumerate(hs):
            sl = slice(h * HEAD, (h + 1) * HEAD)
            o_ref[bi, :, sl] = (normed[j * c:(j + 1) * c, :] * _silu(gz_ref[bi, :, sl])).astype(o_ref.dtype)


def _gdn(raw, ab, gz, conv_w, conv0, s0, a_log, dt_bias, out_gain, b, s, c, n_valid):
    nh = s0.shape[1]
    ch = raw.shape[2]
    bb = math.gcd(b, 4) if (s == c and c * nh <= GDN_CHUNK) else 1
    pad_lane = lambda a: jnp.pad(a.astype(F32), (0, HEAD - a.shape[0])).reshape(1, HEAD)
    out, state = pl.pallas_call(
        functools.partial(_gdn_kernel, c=c, n_valid=n_valid),
        grid=(b // bb, s // c),
        in_specs=[pl.BlockSpec((bb, c, ch), lambda bi, n: (bi, n, 0)),
                  pl.BlockSpec((bb, c, HEAD), lambda bi, n: (bi, n, 0)),
                  pl.BlockSpec((bb, c, nh * HEAD), lambda bi, n: (bi, n, 0)),
                  pl.BlockSpec((GDN_CONV, ch), lambda bi, n: (0, 0)),
                  pl.BlockSpec((bb, GDN_CONV - 1, ch), lambda bi, n: (bi, 0, 0)),
                  pl.BlockSpec((bb, nh, HEAD, HEAD), lambda bi, n: (bi, 0, 0, 0)),
                  pl.BlockSpec((1, HEAD), lambda bi, n: (0, 0)),
                  pl.BlockSpec((1, HEAD), lambda bi, n: (0, 0)),
                  pl.BlockSpec((1, HEAD), lambda bi, n: (0, 0))],
        out_specs=[pl.BlockSpec((bb, c, nh * HEAD), lambda bi, n: (bi, n, 0)),
                   pl.BlockSpec((bb, nh, HEAD, HEAD), lambda bi, n: (bi, 0, 0, 0))],
        out_shape=[jax.ShapeDtypeStruct((b, s, nh * HEAD), BF16),
                   jax.ShapeDtypeStruct(s0.shape, F32)],
        scratch_shapes=[pltpu.VMEM((bb, 8 + c, ch), F32)],
        compiler_params=_cparams(("parallel", "arbitrary"), 32),
        name="gdn",
    )(raw, ab, gz, conv_w, conv0, s0, pad_lane(a_log), pad_lane(dt_bias), out_gain.reshape(1, HEAD))
    return out, state


def _mix_out_kernel(*refs, n_a):
    a_refs = refs[:n_a]
    w_ref, x_ref, g_ref, wq_ref, qg_ref, x_out, q_out = refs[n_a:]
    acc = x_ref[...]
    off = 0
    for a in a_refs:
        ka = a.shape[1]
        acc = acc + _dot(a[...].astype(BF16), w_ref[off:off + ka, :])
        off += ka
    x_out[...] = acc
    xq = _dot(_rms(acc, g_ref[...]).astype(BF16), wq_ref[...])
    qg = qg_ref[...]
    for s in range(xq.shape[1] // HEAD):
        sl = slice(s * HEAD, (s + 1) * HEAD)
        q_out[:, sl] = _rms(xq[:, sl], qg).astype(q_out.dtype)


def _mix_out(acts, w_bf16, x, next_gain, wq_bf16, q_head_gain):
    t, d = x.shape
    qw = wq_bf16.shape[1]
    tm = min(t, 256)
    return pl.pallas_call(
        functools.partial(_mix_out_kernel, n_a=len(acts)),
        grid=(t // tm,),
        in_specs=[pl.BlockSpec((tm, a.shape[1]), lambda i: (i, 0)) for a in acts]
        + [pl.BlockSpec(w_bf16.shape, lambda i: (0, 0)),
           pl.BlockSpec((tm, d), lambda i: (i, 0)),
           pl.BlockSpec((1, d), lambda i: (0, 0)),
           pl.BlockSpec(wq_bf16.shape, lambda i: (0, 0)),
           pl.BlockSpec((1, HEAD), lambda i: (0, 0))],
        out_specs=[pl.BlockSpec((tm, d), lambda i: (i, 0)), pl.BlockSpec((tm, qw), lambda i: (i, 0))],
        out_shape=[jax.ShapeDtypeStruct((t, d), F32), jax.ShapeDtypeStruct((t, qw), BF16)],
        compiler_params=_cparams(("parallel",), 48),
        name="mix_out",
    )(*acts, w_bf16, x, next_gain.reshape(1, d), wq_bf16, q_head_gain.reshape(1, HEAD))


def _xattn_prompt_kernel(q_ref, k_ref, v_ref, o_ref, *, scale):
    for h in range(q_ref.shape[2] // HEAD):
        sl = slice(h * HEAD, (h + 1) * HEAD)
        logits = _dot_nt(q_ref[0, :, sl], k_ref[0, :, sl].astype(BF16)) * scale
        p = jnp.exp(logits - jnp.max(logits, axis=-1, keepdims=True))
        attn = p / jnp.sum(p, axis=-1, keepdims=True)
        o_ref[0, :, sl] = _dot(attn.astype(BF16), v_ref[0, :, sl].astype(BF16)).astype(o_ref.dtype)


def _xattn_prompt(xq, mem_k, mem_v, b, s):
    width = xq.shape[1]
    m = mem_k.shape[1]
    tq = min(s, 512)
    out = pl.pallas_call(
        functools.partial(_xattn_prompt_kernel, scale=HEAD ** -0.5),
        grid=(b, s // tq),
        in_specs=[pl.BlockSpec((1, tq, width), lambda bi, i: (bi, i, 0)),
                  pl.BlockSpec((1, m, width), lambda bi, i: (bi, 0, 0)),
                  pl.BlockSpec((1, m, width), lambda bi, i: (bi, 0, 0))],
        out_specs=pl.BlockSpec((1, tq, width), lambda bi, i: (bi, i, 0)),
        out_shape=jax.ShapeDtypeStruct((b, s, width), BF16),
        compiler_params=_cparams(("parallel", "arbitrary"), 32),
        name="xattn_prompt",
    )(xq.reshape(b, s, width), mem_k, mem_v)
    return out.reshape(b * s, width)


def _xattn_sample_kernel(q_ref, k_ref, v_ref, o_ref, *, n_new, scale):
    bb, nh = q_ref.shape[0], q_ref.shape[1]
    m = k_ref.shape[2] // nh
    pairs = [(i, h) for i in range(bb) for h in range(nh)]
    logits = [_dot_nt(q_ref[i, h].astype(BF16), k_ref[0, i, pl.ds(h, m, stride=nh), :].astype(BF16)) * scale
              for i, h in pairs]
    ps = [jnp.exp(lg - jnp.max(lg, axis=-1, keepdims=True)) for lg in logits]
    attns = [(p / jnp.sum(p, axis=-1, keepdims=True)).astype(BF16) for p in ps]
    for (i, h), attn in zip(pairs, attns):
        o_h = _dot(attn, v_ref[0, i, pl.ds(h, m, stride=nh), :].astype(BF16))
        o_ref[i, :, h * HEAD:(h + 1) * HEAD] = o_h[0:n_new, :]


def _xattn_sample(xq, mem_k, mem_v, layer, b, s):
    width = xq.shape[1]
    nh = width // HEAD
    m = mem_k.shape[2]
    qrows = 8
    bb = math.gcd(b, 8)
    q4 = jnp.pad(xq.reshape(b, s, nh, HEAD).transpose(0, 2, 1, 3), ((0, 0), (0, 0), (0, qrows - s), (0, 0)))
    mem_spec = pl.BlockSpec((1, bb, m * nh, HEAD), lambda bi: (layer, bi, 0, 0))
    flat = lambda c: c.reshape(c.shape[0], c.shape[1], m * nh, HEAD)
    out = pl.pallas_call(
        functools.partial(_xattn_sample_kernel, n_new=s, scale=HEAD ** -0.5),
        grid=(b // bb,),
        in_specs=[pl.BlockSpec((bb, nh, qrows, HEAD), lambda bi: (bi, 0, 0, 0)), mem_spec, mem_spec],
        out_specs=pl.BlockSpec((bb, s, width), lambda bi: (bi, 0, 0)),
        out_shape=jax.ShapeDtypeStruct((b, s, width), F32),
        compiler_params=_cparams(("parallel",), 32),
        name="xattn_sample",
    )(q4, flat(mem_k), flat(mem_v))
    return out.reshape(b * s, width)


def _ffn_kernel(x_ref, a_ref, wa_ref, g_ref, wg_ref, wu_ref, wd_ref, o_ref, h_ref):
    j = pl.program_id(1)

    @pl.when(j == 0)
    def _():
        x = x_ref[...] + _dot(a_ref[...].astype(BF16), wa_ref[...])
        h_ref[...] = _rms(x, g_ref[...]).astype(h_ref.dtype)
        o_ref[...] = x

    h = h_ref[...]
    gate = _dot(h, wg_ref[...].astype(BF16))
    up = _dot(h, wu_ref[...].astype(BF16))
    act = (_silu(gate) * up).astype(BF16)
    o_ref[...] += _dot(act, wd_ref[...].astype(BF16))


def _ffn(x, attn, w_attn_bf16, gain, w_gate_up, w_down):
    t, d = x.shape
    f = w_down.shape[0]
    tm = min(t, 1024)
    tf = 256 if tm == 1024 else 512
    assert f % tf == 0 and t % tm == 0
    nf = f // tf
    return pl.pallas_call(
        _ffn_kernel,
        grid=(t // tm, nf),
        in_specs=[pl.BlockSpec((tm, d), lambda i, j: (i, 0)),
                  pl.BlockSpec((tm, attn.shape[1]), lambda i, j: (i, 0)),
                  pl.BlockSpec(w_attn_bf16.shape, lambda i, j: (0, 0)),
                  pl.BlockSpec((1, d), lambda i, j: (0, 0)),
                  pl.BlockSpec((d, tf), lambda i, j: (0, j)),
                  pl.BlockSpec((d, tf), lambda i, j: (0, nf + j)),
                  pl.BlockSpec((tf, d), lambda i, j: (j, 0))],
        out_specs=pl.BlockSpec((tm, d), lambda i, j: (i, 0)),
        out_shape=jax.ShapeDtypeStruct((t, d), F32),
        scratch_shapes=[pltpu.VMEM((tm, d), BF16)],
        compiler_params=_cparams(("parallel", "arbitrary"), 60),
        name="ffn",
    )(x, attn, w_attn_bf16, gain.reshape(1, d), w_gate_up, w_gate_up, w_down)


def _decoder_layer(x, b, s, p, sb_attend, xattend, conv0, s0):
    d = x.shape[1]
    sbw = SB_HEADS * HEAD
    gw = GDN_HEADS * HEAD
    h = _rmsnorm_bf16(x, p["norm_mix_g"])
    q, k_new, v_new, raw, gz, ab = _in_proj(
        h, p["w_in_t"], p["sb_q_norm_g"] * (HEAD ** -0.5), p["sb_k_norm_g"], p["sb_q_dtype"],
        (sbw, sbw, sbw, 3 * gw), gw, 2 * GDN_HEADS)
    o_sb = sb_attend(q, k_new, v_new)

    c = math.gcd(s, GDN_CHUNK)
    raw3, ab3, gz3 = raw.reshape(b, s, 3 * gw), ab.reshape(b, s, ab.shape[1]), gz.reshape(b, s, gw)
    conv_new = raw3[:, s - (GDN_CONV - 1):, :]
    n_valid = c
    if c < 8:
        padn = ((0, 0), (0, 8 - s), (0, 0))
        raw3, ab3, gz3 = jnp.pad(raw3, padn), jnp.pad(ab3, padn), jnp.pad(gz3, padn)
        c = 8
    o_gdn, state = _gdn(raw3, ab3, gz3, p["gdn_conv_w"], conv0, s0, p["gdn_a_log"], p["gdn_dt_bias"],
                        p["gdn_out_norm_g"], b, raw3.shape[1], c, n_valid)
    o_gdn = o_gdn[:, :s, :].reshape(b * s, gw)

    x1, xq = _mix_out([o_sb, o_gdn], p["w_out_bf"], x, p["norm_x_g"], p["w_xq_bf"], p["x_q_norm_g"])
    xo = xattend(xq)
    y = _ffn(x1, xo, p["w_xo_bf"], p["norm_ffn_g"], p["w_gate_up"], p["w_down"])
    return y, k_new, v_new, state, conv_new


def kernel(x_prompt, x_sample, mem_prompt, cache_sb_k, cache_sb_v, page_table, state_gdn, state_gdn_conv, cache_mem_k, cache_mem_v, norm_mix_g, w_in, sb_q_norm_g, sb_k_norm_g, sb_logit_bias, gdn_conv_w, gdn_a_log, gdn_dt_bias, gdn_out_norm_g, w_out, norm_x_g, norm_mem_g, w_xq, w_mk, w_mv, x_q_norm_g, x_k_norm_g, w_xo, norm_ffn_g, w_gate_up, w_down):
    bp, sp_, d = x_prompt.shape
    bs, ss, _ = x_sample.shape
    depth = w_in.shape[0]
    m = mem_prompt.shape[1]
    sbw = SB_HEADS * HEAD
    gw = GDN_HEADS * HEAD
    xw = X_HEADS * HEAD
    ch = 3 * gw
    ab0 = 3 * sbw + ch
    yp = x_prompt.reshape(bp * sp_, d)
    ys = x_sample.reshape(bs * ss, d)
    outs = [[] for _ in range(10)]
    for l in range(depth):
        p = {
            "norm_mix_g": norm_mix_g[l], "w_in_t": jnp.swapaxes(w_in[l], 0, 1), "sb_q_norm_g": sb_q_norm_g[l],
            "sb_k_norm_g": sb_k_norm_g[l],
            "gdn_conv_w": gdn_conv_w[l], "gdn_a_log": gdn_a_log[l], "gdn_dt_bias": gdn_dt_bias[l],
            "gdn_out_norm_g": gdn_out_norm_g[l], "norm_x_g": norm_x_g[l], "x_q_norm_g": x_q_norm_g[l],
            "norm_ffn_g": norm_ffn_g[l], "w_gate_up": w_gate_up[l], "w_down": w_down[l],
            "w_out_bf": w_out[l].astype(BF16), "w_xq_bf": w_xq[l].astype(BF16), "w_xo_bf": w_xo[l].astype(BF16),
        }
        bias = sb_logit_bias[l]
        hm = _rmsnorm_bf16(mem_prompt.reshape(bp * m, d), norm_mem_g[l])
        mem_k = _proj(hm, w_mk[l], 0, xw, F32, x_k_norm_g[l]).reshape(bp, m, xw)
        mem_v = _proj(hm, w_mv[l], 0, xw, F32).reshape(bp, m, xw)

        pp = dict(p, sb_q_dtype=BF16)
        yp, kp, vp, stp, cvp = _decoder_layer(
            yp, bp, sp_, pp,
            lambda q, k, v: _sb_prompt(q, k, v, bias, bp, sp_),
            lambda xq: _xattn_prompt(xq, mem_k, mem_v, bp, sp_),
            jnp.zeros((bp, GDN_CONV - 1, ch), F32), jnp.zeros((bp, GDN_HEADS, HEAD, HEAD), F32))

        ps = dict(p, sb_q_dtype=F32)
        ys, ks, vs, sts, cvs = _decoder_layer(
            ys, bs, ss, ps,
            lambda q, k, v: _sb_sample(q, k, v, bias, cache_sb_k, cache_sb_v, l, page_table, bs, ss),
            lambda xq: _xattn_sample(xq, cache_mem_k, cache_mem_v, l, bs, ss),
            state_gdn_conv[l], state_gdn[l])

        for acc, val in zip(outs, (
                kp.reshape(bp, sp_, SB_HEADS, HEAD), vp.reshape(bp, sp_, SB_HEADS, HEAD),
                ks.reshape(bs, ss, SB_HEADS, HEAD), vs.reshape(bs, ss, SB_HEADS, HEAD),
                stp.astype(state_gdn.dtype), cvp, sts.astype(state_gdn.dtype), cvs,
                mem_k.reshape(bp, m, X_HEADS, HEAD), mem_v.reshape(bp, m, X_HEADS, HEAD))):
            acc.append(val)
    return (yp.reshape(bp, sp_, d), ys.reshape(bs, ss, d)) + tuple(jnp.stack(o) for o in outs)
```

```python
import functools
import math

import jax
import jax.numpy as jnp
from jax import lax
from jax.experimental import pallas as pl
from jax.experimental.pallas import tpu as pltpu

F32 = jnp.float32
BF16 = jnp.bfloat16
EPS = 1e-6
HEAD = 128
SB_HEADS = 8
GDN_HEADS = 8
X_HEADS = 4
PAGE = 128
GDN_CONV = 4
GDN_CHUNK = 64
MIB = 1 << 20


def _cparams(sem, vmem_mib):
    return pltpu.CompilerParams(dimension_semantics=sem, vmem_limit_bytes=vmem_mib * MIB)


def _dot(a, b):
    return jnp.dot(a, b, preferred_element_type=F32)


def _dot_nt(a, b):
    return lax.dot_general(a, b, (((1,), (1,)), ((), ())), preferred_element_type=F32)


def _dot_tn(a, b):
    return lax.dot_general(a, b, (((0,), (0,)), ((), ())), preferred_element_type=F32)


def _split2(a):
    hi = a.astype(BF16)
    lo = (a - hi.astype(F32)).astype(BF16)
    return hi, lo


def _dot_bf(a, b, dot=_dot):
    return dot(a.astype(BF16), b.astype(BF16))


def _dot_exact_lhs(m01, b, terms=3):
    acc = None
    for _ in range(terms):
        piece = b.astype(BF16)
        b = b - piece.astype(F32)
        part = _dot(m01, piece)
        acc = part if acc is None else acc + part
    return acc


def _dot_exact_rhs(a, m_twice):
    hi, lo = _split2(a)
    return _dot(jnp.concatenate([hi, lo], axis=1), m_twice)


def _softplus(z):
    return jnp.maximum(z, 0.0) + jnp.log1p(jnp.exp(-jnp.abs(z)))


def _softplus_fast(z):
    return jnp.maximum(z, 0.0) + jnp.log(1.0 + jnp.exp(-jnp.abs(z)))


def _silu(x):
    return x * jax.nn.sigmoid(x)


def _rms(y, g):
    ms = jnp.mean(y * y, axis=-1, keepdims=True)
    return y * lax.rsqrt(ms + EPS) * g


def _rmsnorm_kernel(x_ref, g_ref, o_ref):
    o_ref[...] = _rms(x_ref[...], g_ref[...]).astype(o_ref.dtype)


def _rmsnorm_bf16(x, g):
    t, d = x.shape
    tm = min(t, 512)
    return pl.pallas_call(
        _rmsnorm_kernel,
        grid=(t // tm,),
        in_specs=[pl.BlockSpec((tm, d), lambda i: (i, 0)), pl.BlockSpec((1, d), lambda i: (0, 0))],
        out_specs=pl.BlockSpec((tm, d), lambda i: (i, 0)),
        out_shape=jax.ShapeDtypeStruct((t, d), BF16),
        compiler_params=_cparams(("parallel",), 32),
        name="rmsnorm",
    )(x, g.reshape(1, d))


def _proj_kernel(x_ref, w_ref, g_ref, o_ref, *, head_norm):
    acc = _dot(x_ref[...], w_ref[...].astype(BF16))
    if head_norm:
        g = g_ref[...]
        for s in range(acc.shape[1] // HEAD):
            sl = slice(s * HEAD, (s + 1) * HEAD)
            o_ref[:, sl] = _rms(acc[:, sl], g).astype(o_ref.dtype)
    else:
        o_ref[...] = acc.astype(o_ref.dtype)


def _proj(h, w, col0, n_cols, out_dtype, head_gain=None):
    t, k = h.shape
    tm = min(t, 2048)
    tn = min(512, n_cols)
    assert col0 % tn == 0 and n_cols % tn == 0 and t % tm == 0
    cb = col0 // tn
    g = jnp.ones((1, HEAD), F32) if head_gain is None else head_gain.reshape(1, HEAD).astype(F32)
    return pl.pallas_call(
        functools.partial(_proj_kernel, head_norm=head_gain is not None),
        grid=(t // tm, n_cols // tn),
        in_specs=[pl.BlockSpec((tm, k), lambda i, j: (i, 0)),
                  pl.BlockSpec((k, tn), lambda i, j: (0, cb + j)),
                  pl.BlockSpec((1, HEAD), lambda i, j: (0, 0))],
        out_specs=pl.BlockSpec((tm, tn), lambda i, j: (i, j)),
        out_shape=jax.ShapeDtypeStruct((t, n_cols), out_dtype),
        compiler_params=_cparams(("parallel", "arbitrary"), 48),
        name="proj",
    )(h, w, g)


def _in_proj_kernel(h_ref, wm_ref, wz_ref, wl_ref, qg_ref, kg_ref,
                    q_ref, k_ref, v_ref, raw_ref, gz_ref, ab_ref, *, edges):
    j = pl.program_id(1)

    def head_normed(acc, g_ref):
        g = g_ref[...]
        return jnp.concatenate(
            [_rms(acc[:, s * HEAD:(s + 1) * HEAD], g) for s in range(acc.shape[1] // HEAD)], axis=1)

    def group(w_ref, out_ref, g_ref=None):
        def run():
            acc = _dot_nt(h_ref[...], w_ref[...].astype(BF16))
            out_ref[...] = (acc if g_ref is None else head_normed(acc, g_ref)).astype(out_ref.dtype)
        return run

    runs = (group(wm_ref, q_ref, qg_ref), group(wm_ref, k_ref, kg_ref), group(wm_ref, v_ref),
            group(wm_ref, raw_ref), group(wz_ref, gz_ref), group(wl_ref, ab_ref))
    for lo, hi, run in zip((0,) + edges[:-1], edges, runs):
        pl.when((j >= lo) & (j < hi))(run)


def _in_proj(h, w_t, q_gain, k_gain, q_dtype, main_widths, gate_width, n_logit):
    t, k = h.shape
    tm = min(t, 2048)
    tn = 256 if t > 1024 else 512
    n_main = sum(main_widths)
    widths = tuple(main_widths) + (gate_width, tn)
    assert all(w % tn == 0 for w in widths) and t % tm == 0 and n_logit <= tn
    nblk = [w // tn for w in widths]
    edges = tuple(int(sum(nblk[:i + 1])) for i in range(len(nblk)))
    starts = (0,) + edges[:-1]
    w_gate = w_t[n_main + n_logit:n_main + n_logit + gate_width]
    w_logit = jnp.pad(w_t[n_main:n_main + n_logit], ((0, tn - n_logit), (0, 0)))

    def swept(lo, n):
        return lambda i, j: (i, jnp.clip(j - lo, 0, n - 1))

    n_main_blk, gate_lo, n_gate_blk = edges[len(main_widths) - 1], starts[-2], nblk[-2]
    dtypes = (q_dtype,) + (F32,) * (len(widths) - 1)
    gain = lambda g: g.reshape(1, HEAD).astype(F32)
    return pl.pallas_call(
        functools.partial(_in_proj_kernel, edges=edges),
        grid=(t // tm, edges[-1]),
        in_specs=[pl.BlockSpec((tm, k), lambda i, j: (i, 0)),
                  pl.BlockSpec((tn, k), lambda i, j: (jnp.minimum(j, n_main_blk - 1), 0)),
                  pl.BlockSpec((tn, k), lambda i, j: (jnp.clip(j - gate_lo, 0, n_gate_blk - 1), 0)),
                  pl.BlockSpec((tn, k), lambda i, j: (0, 0), pipeline_mode=pl.Buffered(1)),
                  pl.BlockSpec((1, HEAD), lambda i, j: (0, 0)),
                  pl.BlockSpec((1, HEAD), lambda i, j: (0, 0))],
        out_specs=[pl.BlockSpec((tm, tn), swept(lo, n)) for lo, n in zip(starts, nblk)],
        out_shape=[jax.ShapeDtypeStruct((t, w), dt) for w, dt in zip(widths, dtypes)],
        compiler_params=_cparams(("parallel", "arbitrary"), 56),
        name="in_proj",
    )(h, w_t, w_gate, w_logit, gain(q_gain), gain(k_gain))


def _minus_later_twice(n):
    row = lax.broadcasted_iota(jnp.int32, (n, n), 0)
    col = lax.broadcasted_iota(jnp.int32, (n, n), 1)
    m = jnp.where(row > col, -1.0, 0.0).astype(BF16)
    return jnp.concatenate([m, m], axis=0)


def _sb_prompt_kernel(bias_ref, q_ref, k_ref, v_ref, o_ref, *, tq, hpb):
    hb = pl.program_id(1)
    i = pl.program_id(2)
    row = lax.broadcasted_iota(jnp.int32, (tq, tq), 0)
    col = lax.broadcasted_iota(jnp.int32, (tq, tq), 1)
    minus_later = _minus_later_twice(tq)
    causal = col < row
    heads = range(hpb)
    qs = [q_ref[0, :, g * HEAD:(g + 1) * HEAD] for g in heads]
    biases = [bias_ref[hb * hpb + g] for g in heads]

    def block(j, carry, masked):
        start = pl.multiple_of(j * tq, tq)
        kb = k_ref[0, pl.ds(start, tq), :].astype(BF16)
        vb = v_ref[0, pl.ds(start, tq), :].astype(BF16)
        zs = [_dot_nt(qs[g], kb[:, g * HEAD:(g + 1) * HEAD]) + biases[g] for g in heads]
        sps = [_softplus_fast(z) for z in zs]
        leaves = [jnp.where(causal, sp, 0.0) if masked else sp for sp in sps]
        survs = [_dot_exact_rhs(lv, minus_later) for lv in leaves]
        out = []
        for g in heads:
            csum, acc = carry[g]
            w = jnp.exp(zs[g] - sps[g] + survs[g] + csum)
            if masked:
                w = jnp.where(causal, w, 0.0)
            acc = acc + _dot(w.astype(BF16), vb[:, g * HEAD:(g + 1) * HEAD])
            out.append((csum - jnp.sum(leaves[g], axis=-1, keepdims=True), acc))
        return tuple(out)

    init = tuple((jnp.zeros((tq, 1), F32), jnp.zeros((tq, HEAD), F32)) for _ in heads)
    carry = block(i, init, True)
    carry = lax.fori_loop(0, i, lambda jj, c: block(i - 1 - jj, c, False), carry)
    for g in heads:
        o_ref[0, :, g * HEAD:(g + 1) * HEAD] = carry[g][1].astype(o_ref.dtype)


def _sb_prompt(q, k, v, bias, b, s):
    width = q.shape[1]
    nh = width // HEAD
    hpb = 8
    tq = min(s, 256)
    assert s % tq == 0 and nh % hpb == 0
    bw = hpb * HEAD
    q3, k3, v3 = (a.reshape(b, s, width) for a in (q, k, v))
    out = pl.pallas_call(
        functools.partial(_sb_prompt_kernel, tq=tq, hpb=hpb),
        grid=(b, nh // hpb, s // tq),
        in_specs=[pl.BlockSpec(memory_space=pltpu.SMEM),
                  pl.BlockSpec((1, tq, bw), lambda bi, h, i: (bi, i, h)),
                  pl.BlockSpec((1, s, bw), lambda bi, h, i: (bi, 0, h)),
                  pl.BlockSpec((1, s, bw), lambda bi, h, i: (bi, 0, h))],
        out_specs=pl.BlockSpec((1, tq, bw), lambda bi, h, i: (bi, i, h)),
        out_shape=jax.ShapeDtypeStruct((b, s, width), BF16),
        compiler_params=_cparams(("parallel", "parallel", "arbitrary"), 48),
        name="sb_prompt",
    )(bias.astype(F32), q3, k3, v3)
    return out.reshape(b * s, width)


def _sb_sample_kernel(pt_ref, bias_ref, q_ref, knew_ref, vnew_ref, *refs, n_pages, n_new):
    del pt_ref
    k_pages = refs[:n_pages]
    v_pages = refs[n_pages:2 * n_pages]
    o_ref = refs[2 * n_pages]
    rows = q_ref.shape[1]
    nh = o_ref.shape[2]
    q = q_ref[0].astype(BF16)
    bias = bias_ref[...]
    minus_later = _minus_later_twice(HEAD)

    def own_head(width):
        r = lax.broadcasted_iota(jnp.int32, (rows, width), 0)
        c = lax.broadcasted_iota(jnp.int32, (rows, width), 1)
        return r, c, (r & (nh - 1)) == (c & (nh - 1))

    r, c, own = own_head(knew_ref.shape[1])
    shift = int(math.log2(nh))
    new_mask = own & ((c >> shift) < (r >> shift)) & ((c >> shift) < n_new)
    page_mask = own_head(k_pages[0].shape[2] * math.gcd(n_pages, 4))[2]
    segs = [(lambda: knew_ref[0], lambda: vnew_ref[0], new_mask)]
    pps = math.gcd(n_pages, 4)
    for p0 in reversed(range(0, n_pages, pps)):
        take = lambda refs, p0=p0: jnp.concatenate(
            [refs[p][0, 0].astype(BF16) for p in range(p0, p0 + pps)], axis=0)
        segs.append((lambda take=take: take(k_pages), lambda take=take: take(v_pages), page_mask))
    n = len(segs)
    zs, sps, leaves, sums = [None] * n, [None] * n, [None] * n, [None] * n
    csum = jnp.zeros((rows, 1), F32)
    acc = jnp.zeros((rows, HEAD), F32)
    for step in range(n + 2):
        if step < n:
            zs[step] = _dot_nt(q, segs[step][0]().astype(BF16)) + bias
        i = step - 1
        if 0 <= i < n:
            sps[i] = _softplus_fast(zs[i])
            leaves[i] = jnp.where(segs[i][2], sps[i], 0.0)
            sums[i] = [_dot_exact_rhs(leaves[i][:, blk * HEAD:(blk + 1) * HEAD], minus_later)
                       for blk in range(zs[i].shape[1] // HEAD)]
        i = step - 2
        if 0 <= i < n:
            pieces = [None] * len(sums[i])
            for blk in reversed(range(len(sums[i]))):
                pieces[blk] = sums[i][blk] + csum
                csum = csum - jnp.sum(leaves[i][:, blk * HEAD:(blk + 1) * HEAD], axis=-1, keepdims=True)
            survive = pieces[0] if len(pieces) == 1 else jnp.concatenate(pieces, axis=1)
            w = jnp.where(segs[i][2], jnp.exp(zs[i] - sps[i] + survive), 0.0)
            acc = acc + _dot(w.astype(BF16), segs[i][1]().astype(BF16))
            zs[i] = sps[i] = leaves[i] = sums[i] = None
    o_ref[0] = acc.reshape(rows // nh, nh, HEAD)


def _sb_sample(q, k_new, v_new, bias, cache_k, cache_v, layer, page_table, b, s):
    width = q.shape[1]
    nh = width // HEAD
    n_pages = page_table.shape[1]
    rows = s * nh
    new_rows = HEAD
    assert rows <= new_rows and cache_k.shape[2] == PAGE and nh & (nh - 1) == 0

    key_rows = lambda a: jnp.pad(a.reshape(b, rows, HEAD), ((0, 0), (0, new_rows - rows), (0, 0)))
    flat = lambda c: c.reshape(c.shape[0], c.shape[1], PAGE * nh, HEAD)
    bias_rows = jnp.tile(bias.astype(F32), s).reshape(rows, 1)

    def page_spec(p):
        return pl.BlockSpec((1, 1, PAGE * nh, HEAD), lambda bi, pt: (layer, pt[bi, p], 0, 0))

    grid_spec = pltpu.PrefetchScalarGridSpec(
        num_scalar_prefetch=1,
        grid=(b,),
        in_specs=[pl.BlockSpec((rows, 1), lambda bi, pt: (0, 0)),
                  pl.BlockSpec((1, rows, HEAD), lambda bi, pt: (bi, 0, 0)),
                  pl.BlockSpec((1, new_rows, HEAD), lambda bi, pt: (bi, 0, 0)),
                  pl.BlockSpec((1, new_rows, HEAD), lambda bi, pt: (bi, 0, 0))]
        + [page_spec(p) for p in range(n_pages)] * 2,
        out_specs=pl.BlockSpec((1, s, nh, HEAD), lambda bi, pt: (bi, 0, 0, 0)),
    )
    out = pl.pallas_call(
        functools.partial(_sb_sample_kernel, n_pages=n_pages, n_new=s),
        grid_spec=grid_spec,
        out_shape=jax.ShapeDtypeStruct((b, s, nh, HEAD), F32),
        compiler_params=_cparams(("arbitrary",), 56),
        name="sb_sample",
    )(page_table, bias_rows, q.reshape(b, rows, HEAD), key_rows(k_new), key_rows(v_new),
      *([flat(cache_k)] * n_pages), *([flat(cache_v)] * n_pages))
    return out.reshape(b * s, width)


def _gdn_kernel(raw_ref, ab_ref, gz_ref, convw_ref, conv0_ref, s0_ref, alog_ref, dtb_ref, gn_ref,
                o_ref, s_ref, pad_ref, *, c, n_valid):
    n = pl.program_id(1)
    bb, nh = s_ref.shape[0], s_ref.shape[1]
    qkw = nh * HEAD

    @pl.when(n == 0)
    def _():
        for bi in range(bb):
            pad_ref[bi, 0:8, :] = jnp.zeros((8, pad_ref.shape[2]), F32)
            pad_ref[bi, 5:8, :] = conv0_ref[bi]
        s_ref[...] = s0_ref[...]

    cw = convw_ref[...]
    rc = lax.broadcasted_iota(jnp.int32, (c, c), 0)
    cc = lax.broadcasted_iota(jnp.int32, (c, c), 1)
    upto_c = (cc <= rc).astype(BF16)
    gn = gn_ref[...]

    hpg = min(nh, max(1, GDN_CHUNK // c))
    rows = hpg * c

    def stack(pieces):
        return pieces[0] if len(pieces) == 1 else jnp.concatenate(pieces, axis=0)

    def unit(x):
        return x * lax.rsqrt(jnp.sum(x * x, axis=-1, keepdims=True) + EPS)

    groups, ops = [], []
    for bi in range(bb):
        pad_ref[bi, 8:8 + c, :] = raw_ref[bi]
        conv = pad_ref[bi, 5:5 + c, :] * cw[0:1, :]
        for j in range(1, GDN_CONV):
            conv = conv + pad_ref[bi, 5 + j:5 + j + c, :] * cw[j:j + 1, :]
        pad_ref[bi, 5:8, :] = pad_ref[bi, 5 + c:8 + c, :]
        xs = _silu(conv)

        ab = ab_ref[bi]
        live = lax.broadcasted_iota(jnp.int32, ab.shape, 0) < n_valid
        g_all = jnp.where(live, -jnp.exp(alog_ref[...]) * _softplus(ab + dtb_ref[...]), 0.0)
        beta_all = jnp.where(live, jax.nn.sigmoid(ab), 0.0)
        cum_all = _dot_exact_lhs(upto_c, g_all)
        cum_last = cum_all[c - 1:c, :]
        ecum_all = jnp.exp(cum_all)
        eend_all = jnp.exp(cum_last - cum_all)
        glast_all = jnp.exp(cum_last)
        for g0 in range(0, nh, hpg):
            hs = list(range(g0, g0 + hpg))
            q = unit(stack([xs[:, h * HEAD:(h + 1) * HEAD] for h in hs])) * (HEAD ** -0.5)
            k = unit(stack([xs[:, qkw + h * HEAD:qkw + (h + 1) * HEAD] for h in hs]))
            v = stack([xs[:, 2 * qkw + h * HEAD:2 * qkw + (h + 1) * HEAD] for h in hs])
            col = lambda a, off: stack([a[:, off + h:off + h + 1] for h in hs])
            groups.append((bi, hs))
            ops.append(dict(q=q, k=k, v=v, g=col(g_all, 0), beta=col(beta_all, nh),
                            ecum=col(ecum_all, 0), eend=col(eend_all, 0), glast=glast_all))

    ri = lax.broadcasted_iota(jnp.int32, (rows, rows), 0)
    ci = lax.broadcasted_iota(jnp.int32, (rows, rows), 1)
    shift = int(math.log2(c))
    same = (ri >> shift) == (ci >> shift)
    strict = same & (ri > ci)
    incl = same & (ri >= ci)
    upto = incl.astype(BF16)
    eye = (ri == ci).astype(F32)
    diffs = [_dot_exact_lhs(upto, jnp.where(strict, jnp.broadcast_to(o["g"], (rows, rows)), 0.0), terms=2)
             for o in ops]
    kks = [_dot_bf(o["k"], o["k"], _dot_nt) for o in ops]
    qks = [_dot_bf(o["q"], o["k"], _dot_nt) for o in ops]
    decays = [jnp.where(incl, jnp.exp(d), 0.0) for d in diffs]
    npows = [jnp.where(strict, o["beta"] * kk * dec, 0.0) for o, kk, dec in zip(ops, kks, decays)]
    invs = [eye - n for n in npows]
    for _ in range(shift - 1):
        npows = [_dot_bf(n, n) for n in npows]
        invs = [inv + _dot_bf(inv, n) for inv, n in zip(invs, npows)]
    sols = [_dot_bf(inv, jnp.concatenate([o["v"] * o["beta"], o["k"] * (o["beta"] * o["ecum"])], axis=-1))
            for inv, o in zip(invs, ops)]
    qdecs = [o["q"] * o["ecum"] for o in ops]
    from_state = []
    for (bi, hs), sol, qdec in zip(groups, sols, qdecs):
        for j, h in enumerate(hs):
            r = slice(j * c, (j + 1) * c)
            from_state.append(_dot_bf(jnp.concatenate([sol[r, HEAD:], qdec[r, :]], axis=0), s_ref[bi, h]))
    v_news, outs = [], []
    for gi, (sol, qk, dec) in enumerate(zip(sols, qks, decays)):
        mine = from_state[gi * hpg:(gi + 1) * hpg]
        v_new = sol[:, :HEAD] - stack([m[:c, :] for m in mine])
        v_news.append(v_new)
        outs.append(stack([m[c:, :] for m in mine]) + _dot_bf(qk * dec, v_new))
    for (bi, hs), o, v_new in zip(groups, ops, v_news):
        k_end = o["k"] * o["eend"]
        for j, h in enumerate(hs):
            r = slice(j * c, (j + 1) * c)
            s_ref[bi, h] = (o["glast"][:, h:h + 1] * s_ref[bi, h]
                            + _dot_bf(k_end[r, :], v_new[r, :], _dot_tn))
    for (bi, hs), out in zip(groups, outs):
        normed = _rms(out, gn)
        for j, h in enumerate(hs):
            sl = slice(h * HEAD, (h + 1) * HEAD)
            o_ref[bi, :, sl] = (normed[j * c:(j + 1) * c, :] * _silu(gz_ref[bi, :, sl])).astype(o_ref.dtype)


def _gdn(raw, ab, gz, conv_w, conv0, s0, a_log, dt_bias, out_gain, b, s, c, n_valid):
    nh = s0.shape[1]
    ch = raw.shape[2]
    bb = math.gcd(b, 4) if (s == c and c * nh <= GDN_CHUNK) else 1
    pad_lane = lambda a: jnp.pad(a.astype(F32), (0, HEAD - a.shape[0])).reshape(1, HEAD)
    out, state = pl.pallas_call(
        functools.partial(_gdn_kernel, c=c, n_valid=n_valid),
        grid=(b // bb, s // c),
        in_specs=[pl.BlockSpec((bb, c, ch), lambda bi, n: (bi, n, 0)),
                  pl.BlockSpec((bb, c, HEAD), lambda bi, n: (bi, n, 0)),
                  pl.BlockSpec((bb, c, nh * HEAD), lambda bi, n: (bi, n, 0)),
                  pl.BlockSpec((GDN_CONV, ch), lambda bi, n: (0, 0)),
                  pl.BlockSpec((bb, GDN_CONV - 1, ch), lambda bi, n: (bi, 0, 0)),
                  pl.BlockSpec((bb, nh, HEAD, HEAD), lambda bi, n: (bi, 0, 0, 0)),
                  pl.BlockSpec((1, HEAD), lambda bi, n: (0, 0)),
                  pl.BlockSpec((1, HEAD), lambda bi, n: (0, 0)),
                  pl.BlockSpec((1, HEAD), lambda bi, n: (0, 0))],
        out_specs=[pl.BlockSpec((bb, c, nh * HEAD), lambda bi, n: (bi, n, 0)),
                   pl.BlockSpec((bb, nh, HEAD, HEAD), lambda bi, n: (bi, 0, 0, 0))],
        out_shape=[jax.ShapeDtypeStruct((b, s, nh * HEAD), BF16),
                   jax.ShapeDtypeStruct(s0.shape, F32)],
        scratch_shapes=[pltpu.VMEM((bb, 8 + c, ch), F32)],
        compiler_params=_cparams(("parallel", "arbitrary"), 32),
        name="gdn",
    )(raw, ab, gz, conv_w, conv0, s0, pad_lane(a_log), pad_lane(dt_bias), out_gain.reshape(1, HEAD))
    return out, state


def _mix_out_kernel(*refs, n_a):
    a_refs = refs[:n_a]
    w_ref, x_ref, g_ref, wq_ref, qg_ref, x_out, q_out = refs[n_a:]
    acc = x_ref[...]
    off = 0
    for a in a_refs:
        ka = a.shape[1]
        acc = acc + _dot(a[...].astype(BF16), w_ref[off:off + ka, :])
        off += ka
    x_out[...] = acc
    xq = _dot(_rms(acc, g_ref[...]).astype(BF16), wq_ref[...])
    qg = qg_ref[...]
    for s in range(xq.shape[1] // HEAD):
        sl = slice(s * HEAD, (s + 1) * HEAD)
        q_out[:, sl] = _rms(xq[:, sl], qg).astype(q_out.dtype)


def _mix_out(acts, w_bf16, x, next_gain, wq_bf16, q_head_gain):
    t, d = x.shape
    qw = wq_bf16.shape[1]
    tm = min(t, 512)
    return pl.pallas_call(
        functools.partial(_mix_out_kernel, n_a=len(acts)),
        grid=(t // tm,),
        in_specs=[pl.BlockSpec((tm, a.shape[1]), lambda i: (i, 0)) for a in acts]
        + [pl.BlockSpec(w_bf16.shape, lambda i: (0, 0)),
           pl.BlockSpec((tm, d), lambda i: (i, 0)),
           pl.BlockSpec((1, d), lambda i: (0, 0)),
           pl.BlockSpec(wq_bf16.shape, lambda i: (0, 0)),
           pl.BlockSpec((1, HEAD), lambda i: (0, 0))],
        out_specs=[pl.BlockSpec((tm, d), lambda i: (i, 0)), pl.BlockSpec((tm, qw), lambda i: (i, 0))],
        out_shape=[jax.ShapeDtypeStruct((t, d), F32), jax.ShapeDtypeStruct((t, qw), BF16)],
        compiler_params=_cparams(("parallel",), 48),
        name="mix_out",
    )(*acts, w_bf16, x, next_gain.reshape(1, d), wq_bf16, q_head_gain.reshape(1, HEAD))


def _xattn_prompt_kernel(q_ref, k_ref, v_ref, o_ref, *, scale):
    for h in range(q_ref.shape[2] // HEAD):
        sl = slice(h * HEAD, (h + 1) * HEAD)
        logits = _dot_nt(q_ref[0, :, sl], k_ref[0, :, sl].astype(BF16)) * scale
        p = jnp.exp(logits - jnp.max(logits, axis=-1, keepdims=True))
        attn = p / jnp.sum(p, axis=-1, keepdims=True)
        o_ref[0, :, sl] = _dot(attn.astype(BF16), v_ref[0, :, sl].astype(BF16)).astype(o_ref.dtype)


def _xattn_prompt(xq, mem_k, mem_v, b, s):
    width = xq.shape[1]
    m = mem_k.shape[1]
    tq = min(s, 512)
    out = pl.pallas_call(
        functools.partial(_xattn_prompt_kernel, scale=HEAD ** -0.5),
        grid=(b, s // tq),
        in_specs=[pl.BlockSpec((1, tq, width), lambda bi, i: (bi, i, 0)),
                  pl.BlockSpec((1, m, width), lambda bi, i: (bi, 0, 0)),
                  pl.BlockSpec((1, m, width), lambda bi, i: (bi, 0, 0))],
        out_specs=pl.BlockSpec((1, tq, width), lambda bi, i: (bi, i, 0)),
        out_shape=jax.ShapeDtypeStruct((b, s, width), BF16),
        compiler_params=_cparams(("parallel", "arbitrary"), 32),
        name="xattn_prompt",
    )(xq.reshape(b, s, width), mem_k, mem_v)
    return out.reshape(b * s, width)


def _xattn_sample_kernel(q_ref, k_ref, v_ref, o_ref, *, n_new, scale):
    bb, nh = q_ref.shape[0], q_ref.shape[1]
    m = k_ref.shape[2] // nh
    pairs = [(i, h) for i in range(bb) for h in range(nh)]
    logits = [_dot_nt(q_ref[i, h].astype(BF16), k_ref[0, i, pl.ds(h, m, stride=nh), :].astype(BF16)) * scale
              for i, h in pairs]
    ps = [jnp.exp(lg - jnp.max(lg, axis=-1, keepdims=True)) for lg in logits]
    attns = [(p / jnp.sum(p, axis=-1, keepdims=True)).astype(BF16) for p in ps]
    for (i, h), attn in zip(pairs, attns):
        o_h = _dot(attn, v_ref[0, i, pl.ds(h, m, stride=nh), :].astype(BF16))
        o_ref[i, :, h * HEAD:(h + 1) * HEAD] = o_h[0:n_new, :]


def _xattn_sample(xq, mem_k, mem_v, layer, b, s):
    width = xq.shape[1]
    nh = width // HEAD
    m = mem_k.shape[2]
    qrows = 8
    bb = math.gcd(b, 8)
    q4 = jnp.pad(xq.reshape(b, s, nh, HEAD).transpose(0, 2, 1, 3), ((0, 0), (0, 0), (0, qrows - s), (0, 0)))
    mem_spec = pl.BlockSpec((1, bb, m * nh, HEAD), lambda bi: (layer, bi, 0, 0))
    flat = lambda c: c.reshape(c.shape[0], c.shape[1], m * nh, HEAD)
    out = pl.pallas_call(
        functools.partial(_xattn_sample_kernel, n_new=s, scale=HEAD ** -0.5),
        grid=(b // bb,),
        in_specs=[pl.BlockSpec((bb, nh, qrows, HEAD), lambda bi: (bi, 0, 0, 0)), mem_spec, mem_spec],
        out_specs=pl.BlockSpec((bb, s, width), lambda bi: (bi, 0, 0)),
        out_shape=jax.ShapeDtypeStruct((b, s, width), F32),
        compiler_params=_cparams(("parallel",), 32),
        name="xattn_sample",
    )(q4, flat(mem_k), flat(mem_v))
    return out.reshape(b * s, width)


def _ffn_kernel(x_ref, a_ref, wa_ref, g_ref, wg_ref, wu_ref, wd_ref, o_ref, h_ref):
    j = pl.program_id(1)

    @pl.when(j == 0)
    def _():
        x = x_ref[...] + _dot(a_ref[...].astype(BF16), wa_ref[...])
        h_ref[...] = _rms(x, g_ref[...]).astype(h_ref.dtype)
        o_ref[...] = x

    h = h_ref[...]
    gate = _dot(h, wg_ref[...].astype(BF16))
    up = _dot(h, wu_ref[...].astype(BF16))
    act = (_silu(gate) * up).astype(BF16)
    o_ref[...] += _dot(act, wd_ref[...].astype(BF16))


def _ffn(x, attn, w_attn_bf16, gain, w_gate_up, w_down):
    t, d = x.shape
    f = w_down.shape[0]
    tm = min(t, 1024)
    tf = 256 if tm == 1024 else 512
    assert f % tf == 0 and t % tm == 0
    nf = f // tf
    return pl.pallas_call(
        _ffn_kernel,
        grid=(t // tm, nf),
        in_specs=[pl.BlockSpec((tm, d), lambda i, j: (i, 0)),
                  pl.BlockSpec((tm, attn.shape[1]), lambda i, j: (i, 0)),
                  pl.BlockSpec(w_attn_bf16.shape, lambda i, j: (0, 0)),
                  pl.BlockSpec((1, d), lambda i, j: (0, 0)),
                  pl.BlockSpec((d, tf), lambda i, j: (0, j)),
                  pl.BlockSpec((d, tf), lambda i, j: (0, nf + j)),
                  pl.BlockSpec((tf, d), lambda i, j: (j, 0))],
        out_specs=pl.BlockSpec((tm, d), lambda i, j: (i, 0)),
        out_shape=jax.ShapeDtypeStruct((t, d), F32),
        scratch_shapes=[pltpu.VMEM((tm, d), BF16)],
        compiler_params=_cparams(("parallel", "arbitrary"), 60),
        name="ffn",
    )(x, attn, w_attn_bf16, gain.reshape(1, d), w_gate_up, w_gate_up, w_down)


def _decoder_layer(x, b, s, p, sb_attend, xattend, conv0, s0):
    d = x.shape[1]
    sbw = SB_HEADS * HEAD
    gw = GDN_HEADS * HEAD
    h = _rmsnorm_bf16(x, p["norm_mix_g"])
    q, k_new, v_new, raw, gz, ab = _in_proj(
        h, p["w_in_t"], p["sb_q_norm_g"] * (HEAD ** -0.5), p["sb_k_norm_g"], p["sb_q_dtype"],
        (sbw, sbw, sbw, 3 * gw), gw, 2 * GDN_HEADS)
    o_sb = sb_attend(q, k_new, v_new)

    c = math.gcd(s, GDN_CHUNK)
    raw3, ab3, gz3 = raw.reshape(b, s, 3 * gw), ab.reshape(b, s, ab.shape[1]), gz.reshape(b, s, gw)
    conv_new = raw3[:, s - (GDN_CONV - 1):, :]
    n_valid = c
    if c < 8:
        padn = ((0, 0), (0, 8 - s), (0, 0))
        raw3, ab3, gz3 = jnp.pad(raw3, padn), jnp.pad(ab3, padn), jnp.pad(gz3, padn)
        c = 8
    o_gdn, state = _gdn(raw3, ab3, gz3, p["gdn_conv_w"], conv0, s0, p["gdn_a_log"], p["gdn_dt_bias"],
                        p["gdn_out_norm_g"], b, raw3.shape[1], c, n_valid)
    o_gdn = o_gdn[:, :s, :].reshape(b * s, gw)

    x1, xq = _mix_out([o_sb, o_gdn], p["w_out_bf"], x, p["norm_x_g"], p["w_xq_bf"], p["x_q_norm_g"])
    xo = xattend(xq)
    y = _ffn(x1, xo, p["w_xo_bf"], p["norm_ffn_g"], p["w_gate_up"], p["w_down"])
    return y, k_new, v_new, state, conv_new


def kernel(x_prompt, x_sample, mem_prompt, cache_sb_k, cache_sb_v, page_table, state_gdn, state_gdn_conv, cache_mem_k, cache_mem_v, norm_mix_g, w_in, sb_q_norm_g, sb_k_norm_g, sb_logit_bias, gdn_conv_w, gdn_a_log, gdn_dt_bias, gdn_out_norm_g, w_out, norm_x_g, norm_mem_g, w_xq, w_mk, w_mv, x_q_norm_g, x_k_norm_g, w_xo, norm_ffn_g, w_gate_up, w_down):
    bp, sp_, d = x_prompt.shape
    bs, ss, _ = x_sample.shape
    depth = w_in.shape[0]
    m = mem_prompt.shape[1]
    gw = GDN_HEADS * HEAD
    xw = X_HEADS * HEAD
    ch = 3 * gw
    yp = x_prompt.reshape(bp * sp_, d)
    ys = x_sample.reshape(bs * ss, d)
    outs = [[] for _ in range(10)]
    for l in range(depth):
        p = {
            "norm_mix_g": norm_mix_g[l], "w_in_t": jnp.swapaxes(w_in[l], 0, 1), "sb_q_norm_g": sb_q_norm_g[l],
            "sb_k_norm_g": sb_k_norm_g[l],
            "gdn_conv_w": gdn_conv_w[l], "gdn_a_log": gdn_a_log[l], "gdn_dt_bias": gdn_dt_bias[l],
            "gdn_out_norm_g": gdn_out_norm_g[l], "norm_x_g": norm_x_g[l], "x_q_norm_g": x_q_norm_g[l],
            "norm_ffn_g": norm_ffn_g[l], "w_gate_up": w_gate_up[l], "w_down": w_down[l],
            "w_out_bf": w_out[l].astype(BF16), "w_xq_bf": w_xq[l].astype(BF16), "w_xo_bf": w_xo[l].astype(BF16),
        }
        bias = sb_logit_bias[l]
        hm = _rmsnorm_bf16(mem_prompt.reshape(bp * m, d), norm_mem_g[l])
        mem_k = _proj(hm, w_mk[l], 0, xw, F32, x_k_norm_g[l]).reshape(bp, m, xw)
        mem_v = _proj(hm, w_mv[l], 0, xw, F32).reshape(bp, m, xw)

        pp = dict(p, sb_q_dtype=BF16)
        yp, kp, vp, stp, cvp = _decoder_layer(
            yp, bp, sp_, pp,
            lambda q, k, v: _sb_prompt(q, k, v, bias, bp, sp_),
            lambda xq: _xattn_prompt(xq, mem_k, mem_v, bp, sp_),
            jnp.zeros((bp, GDN_CONV - 1, ch), F32), jnp.zeros((bp, GDN_HEADS, HEAD, HEAD), F32))

        ps = dict(p, sb_q_dtype=F32)
        ys, ks, vs, sts, cvs = _decoder_layer(
            ys, bs, ss, ps,
            lambda q, k, v: _sb_sample(q, k, v, bias, cache_sb_k, cache_sb_v, l, page_table, bs, ss),
            lambda xq: _xattn_sample(xq, cache_mem_k, cache_mem_v, l, bs, ss),
            state_gdn_conv[l], state_gdn[l])

        for acc, val in zip(outs, (
                kp.reshape(bp, sp_, SB_HEADS, HEAD), vp.reshape(bp, sp_, SB_HEADS, HEAD),
                ks.reshape(bs, ss, SB_HEADS, HEAD), vs.reshape(bs, ss, SB_HEADS, HEAD),
                stp.astype(state_gdn.dtype), cvp, sts.astype(state_gdn.dtype), cvs,
                mem_k.reshape(bp, m, X_HEADS, HEAD), mem_v.reshape(bp, m, X_HEADS, HEAD))):
            acc.append(val)
    return (yp.reshape(bp, sp_, d), ys.reshape(bs, ss, d)) + tuple(jnp.stack(o) for o in outs)
```
